```python
import math
import jax, jax.numpy as jnp
from jax import lax
import numpy as np

D_MODEL = 1024
BATCH = 4
SEQ = 8192
DEPTH = 2

EPS = 1e-6
SC_WIDTH = 1024
SC_CONV_WIDTH = 3
SSM_D_INNER = 1024
SSM_HEADDIM = 64
SSM_HEADS = SSM_D_INNER // SSM_HEADDIM
SSM_GROUPS = 4
SSM_HPG = SSM_HEADS // SSM_GROUPS
SSM_STATE = 128
SSM_CONV_WIDTH = 4
SSM_CHUNK = 128
SSM_CONV_DIM = SSM_D_INNER + 2 * SSM_GROUPS * SSM_STATE
GLA_HEADS = 4
GLA_KEY_DIM = D_MODEL // 2
GLA_VAL_DIM = D_MODEL
GLA_HEAD_K = GLA_KEY_DIM // GLA_HEADS
GLA_HEAD_V = GLA_VAL_DIM // GLA_HEADS
GLA_GATE_RANK = 16
GLA_GATE_NORMALIZER = 16.0
GLA_CHUNK = 64
N_BRANCHES = 3
D_FF = 4 * D_MODEL

SPLIT_SIZES = (
    SC_WIDTH, SC_WIDTH, SC_WIDTH,
    SSM_D_INNER, SSM_CONV_DIM, SSM_HEADS,
    GLA_KEY_DIM, GLA_KEY_DIM, GLA_VAL_DIM, GLA_VAL_DIM, GLA_GATE_RANK,
    N_BRANCHES * D_MODEL,
)
IN_WIDTH = 12320

kernel_name = "hybrid_conv_ssd_gla_block"


def split_cols(u, sizes):
    outs, start = [], 0
    for s in sizes:
        outs.append(u[..., start:start + s])
        start += s
    return outs


def rms_norm(x, w):
    xf = x.astype(jnp.float32)
    y = xf * lax.rsqrt(jnp.mean(xf * xf, axis=-1, keepdims=True) + EPS)
    return (y * w.astype(jnp.float32)).astype(x.dtype)


def causal_dwconv(u, w):
    K = w.shape[0]
    s = u.shape[1]
    up = jnp.pad(u, ((0, 0), (K - 1, 0), (0, 0)))
    y = up[:, 0:s, :] * w[0]
    for k in range(1, K):
        y = y + up[:, k:k + s, :] * w[k]
    return y


def short_conv_mixer(a_x, a_b, a_c, conv_w, w_out):
    u = causal_dwconv(a_c * a_x, conv_w)
    return (a_b * u) @ w_out


def mamba2_ssd(z, xbc, dt_raw, conv_w, conv_b, dt_bias, a_log, d_skip, norm_w, w_out):
    b, s, _ = z.shape
    f32 = jnp.float32
    G, R, P, N, Q = SSM_GROUPS, SSM_HPG, SSM_HEADDIM, SSM_STATE, SSM_CHUNK
    nc = s // Q
    xbc = jax.nn.silu(causal_dwconv(xbc, conv_w) + conv_b)
    xs, bm, cm = split_cols(xbc, (SSM_D_INNER, G * N, G * N))
    dt = jax.nn.softplus((dt_raw + dt_bias).astype(f32))
    a = -jnp.exp(a_log.astype(f32)).reshape(G, R)
    xh = xs.astype(f32).reshape(b, nc, Q, G, R, P)
    bm = bm.astype(f32).reshape(b, nc, Q, G, N)
    cm = cm.astype(f32).reshape(b, nc, Q, G, N)
    dt_c = dt.reshape(b, nc, Q, G, R)
    a_cum = jnp.cumsum(dt_c * a, axis=2)
    xdt = xh * dt_c[..., None]
    causal = jnp.tril(jnp.ones((Q, Q), dtype=bool))[None, None, :, :, None, None]
    seg = a_cum[:, :, :, None] - a_cum[:, :, None]
    decay_ls = jnp.where(causal, jnp.exp(jnp.where(causal, seg, 0.0)), 0.0)
    cb = jnp.einsum("bclgn,bcsgn->bclsg", cm, bm)
    y_diag = jnp.einsum("bclsg,bclsgr,bcsgrp->bclgrp", cb, decay_ls, xdt)
    decay_to_end = jnp.exp(a_cum[:, :, -1:] - a_cum)
    states = jnp.einsum("bclgn,bclgr,bclgrp->bcgrpn", bm, decay_to_end, xdt)
    chunk_decay = jnp.exp(a_cum[:, :, -1])

    def step(carry, inp):
        st, dec = inp
        return dec[..., None, None] * carry + st, carry

    init = jnp.zeros((b, G, R, P, N), f32)
    _, prev = lax.scan(step, init, (jnp.moveaxis(states, 1, 0), jnp.moveaxis(chunk_decay, 1, 0)))
    prev = jnp.moveaxis(prev, 0, 1)
    y_off = jnp.einsum("bclgn,bcgrpn,bclgr->bclgrp", cm, prev, jnp.exp(a_cum))
    y = y_diag + y_off + xh * d_skip.astype(f32).reshape(G, R)[:, :, None]
    y = y.reshape(b, s, SSM_D_INNER) * jax.nn.silu(z.astype(f32))
    yg = y.reshape(b, s, G, SSM_D_INNER // G)
    yg = yg * lax.rsqrt(jnp.mean(yg * yg, axis=-1, keepdims=True) + EPS)
    y = yg.reshape(b, s, SSM_D_INNER) * norm_w.astype(f32)
    return y.astype(z.dtype) @ w_out


def gla_mixer(q, k, v, g, gk_low, w_gk2, b_gk, norm_w, w_out):
    b, s, _ = q.shape
    f32 = jnp.float32
    H, DK, DV, T = GLA_HEADS, GLA_HEAD_K, GLA_HEAD_V, GLA_CHUNK
    nc = s // T
    gk = jax.nn.log_sigmoid((gk_low @ w_gk2 + b_gk).astype(f32)) / GLA_GATE_NORMALIZER
    qh = q.astype(f32).reshape(b, nc, T, H, DK) * (DK ** -0.5)
    kh = k.astype(f32).reshape(b, nc, T, H, DK)
    vh = v.astype(f32).reshape(b, nc, T, H, DV)
    gcum = jnp.cumsum(gk.reshape(b, nc, T, H, DK), axis=2)
    q_in = qh * jnp.exp(gcum)
    k_in = kh * jnp.exp(-gcum)
    causal = jnp.tril(jnp.ones((T, T), dtype=bool))[None, None, None]
    scores = jnp.where(causal, jnp.einsum("bclhd,bcshd->bchls", q_in, k_in), 0.0)
    o_intra = jnp.einsum("bchls,bcshv->bclhv", scores, vh)
    g_last = gcum[:, :, -1]
    k_end = kh * jnp.exp(g_last[:, :, None] - gcum)
    chunk_kv = jnp.einsum("bclhd,bclhv->bchdv", k_end, vh)
    chunk_decay = jnp.exp(g_last)

    def step(carry, inp):
        kv, dec = inp
        return dec[..., None] * carry + kv, carry

    init = jnp.zeros((b, H, DK, DV), f32)
    _, prev = lax.scan(step, init, (jnp.moveaxis(chunk_kv, 1, 0), jnp.moveaxis(chunk_decay, 1, 0)))
    prev = jnp.moveaxis(prev, 0, 1)
    o_inter = jnp.einsum("bclhd,bchdv->bclhv", q_in, prev)
    o = (o_intra + o_inter).reshape(b, s, H, DV)
    o = o * lax.rsqrt(jnp.mean(o * o, axis=-1, keepdims=True) + EPS) * norm_w.astype(f32)
    o = o * jax.nn.silu(g.astype(f32).reshape(b, s, H, DV))
    return o.reshape(b, s, GLA_VAL_DIM).astype(q.dtype) @ w_out


def setup_inputs(seed: int = 0) -> dict:
    key = jax.random.key(seed)
    ks = jax.random.split(key, 24)
    f32 = jnp.float32

    def nrm(k, shape, scale):
        return jax.random.normal(k, shape, f32) * scale

    dt0 = jnp.exp(jax.random.uniform(ks[7], (DEPTH, SSM_HEADS), f32, math.log(1e-3), math.log(1e-1)))
    return {
        "x": nrm(ks[0], (BATCH, SEQ, D_MODEL), 1.0),
        "norm_mix_w": 1.0 + nrm(ks[1], (DEPTH, D_MODEL), 0.02),
        "w_in": nrm(ks[2], (DEPTH, D_MODEL, IN_WIDTH), D_MODEL ** -0.5),
        "conv_a_w": nrm(ks[3], (DEPTH, SC_CONV_WIDTH, SC_WIDTH), SC_CONV_WIDTH ** -0.5),
        "w_out_a": nrm(ks[4], (DEPTH, SC_WIDTH, D_MODEL), SC_WIDTH ** -0.5),
        "ssm_conv_w": nrm(ks[5], (DEPTH, SSM_CONV_WIDTH, SSM_CONV_DIM), SSM_CONV_WIDTH ** -0.5),
        "ssm_conv_b": nrm(ks[6], (DEPTH, SSM_CONV_DIM), 0.02),
        "ssm_dt_bias": dt0 + jnp.log(-jnp.expm1(-dt0)),
        "ssm_a_log": jnp.log(jax.random.uniform(ks[8], (DEPTH, SSM_HEADS), f32, 1.0, 16.0)),
        "ssm_d": 1.0 + nrm(ks[9], (DEPTH, SSM_HEADS), 0.1),
        "ssm_norm_w": 1.0 + nrm(ks[10], (DEPTH, SSM_D_INNER), 0.02),
        "w_out_ssm": nrm(ks[11], (DEPTH, SSM_D_INNER, D_MODEL), SSM_D_INNER ** -0.5),
        "gla_w_gk2": nrm(ks[12], (DEPTH, GLA_GATE_RANK, GLA_KEY_DIM), GLA_GATE_RANK ** -0.5),
        "gla_b_gk": nrm(ks[13], (DEPTH, GLA_KEY_DIM), 0.1),
        "gla_norm_w": 1.0 + nrm(ks[14], (DEPTH, GLA_HEAD_V), 0.02),
        "w_out_gla": nrm(ks[15], (DEPTH, GLA_VAL_DIM, D_MODEL), GLA_VAL_DIM ** -0.5),
        "w_o": nrm(ks[16], (DEPTH, D_MODEL, D_MODEL), D_MODEL ** -0.5),
        "norm_mlp_w": 1.0 + nrm(ks[17], (DEPTH, D_MODEL), 0.02),
        "w_mlp_up": nrm(ks[18], (DEPTH, D_MODEL, D_FF), D_MODEL ** -0.5),
        "w_mlp_down": nrm(ks[19], (DEPTH, D_FF, D_MODEL), D_FF ** -0.5),
        "norm_f_w": 1.0 + nrm(ks[20], (D_MODEL,), 0.02),
    }


def reference(x, norm_mix_w, w_in, conv_a_w, w_out_a, ssm_conv_w, ssm_conv_b, ssm_dt_bias,
              ssm_a_log, ssm_d, ssm_norm_w, w_out_ssm, gla_w_gk2, gla_b_gk, gla_norm_w,
              w_out_gla, w_o, norm_mlp_w, w_mlp_up, w_mlp_down, norm_f_w):
    b, s, _ = x.shape
    for i in range(DEPTH):
        h = rms_norm(x, norm_mix_w[i])
        u = h @ w_in[i]
        (a_x, a_b, a_c, z, xbc, dt_raw, q, k, v, g, gk_low, gates) = split_cols(u, SPLIT_SIZES)
        y_a = short_conv_mixer(a_x, a_b, a_c, conv_a_w[i], w_out_a[i])
        y_b = mamba2_ssd(z, xbc, dt_raw, ssm_conv_w[i], ssm_conv_b[i], ssm_dt_bias[i],
                         ssm_a_log[i], ssm_d[i], ssm_norm_w[i], w_out_ssm[i])
        y_c = gla_mixer(q, k, v, g, gk_low, gla_w_gk2[i], gla_b_gk[i], gla_norm_w[i], w_out_gla[i])
        gt = jax.nn.sigmoid(gates.astype(jnp.float32)).reshape(b, s, N_BRANCHES, D_MODEL)
        merged = (gt[:, :, 0] * y_a.astype(jnp.float32) + gt[:, :, 1] * y_b.astype(jnp.float32)
                  + gt[:, :, 2] * y_c.astype(jnp.float32))
        x = x + merged.astype(x.dtype) @ w_o[i]
        h = rms_norm(x, norm_mlp_w[i])
        x = x + jnp.square(jax.nn.relu(h @ w_mlp_up[i])) @ w_mlp_down[i]
    return rms_norm(x, norm_f_w)
```

```python
import functools
import math

import jax
import jax.numpy as jnp
from jax import lax
from jax.experimental import pallas as pl
from jax.experimental.pallas import tpu as pltpu

F32 = jnp.float32
BF16 = jnp.bfloat16

EPS = 1e-6
D_MODEL = 1024
SC_WIDTH = 1024
SC_CONV_WIDTH = 3
SSM_D_INNER = 1024
SSM_HEADDIM = 64
SSM_HEADS = SSM_D_INNER // SSM_HEADDIM
SSM_GROUPS = 4
SSM_HPG = SSM_HEADS // SSM_GROUPS
SSM_STATE = 128
SSM_CONV_WIDTH = 4
SSM_CHUNK = 128
SSM_CONV_DIM = SSM_D_INNER + 2 * SSM_GROUPS * SSM_STATE
SSM_GROUP_WIDTH = SSM_D_INNER // SSM_GROUPS
GLA_HEADS = 4
GLA_KEY_DIM = D_MODEL // 2
GLA_VAL_DIM = D_MODEL
GLA_HEAD_K = GLA_KEY_DIM // GLA_HEADS
GLA_HEAD_V = GLA_VAL_DIM // GLA_HEADS
GLA_GATE_RANK = 16
GLA_GATE_NORMALIZER = 16.0
GLA_CHUNK = 64
N_BRANCHES = 3
D_FF = 4 * D_MODEL

LANES = 128
BF16_SUBLANES = 16
HALO_ROWS = BF16_SUBLANES
VMEM_LIMIT_BYTES = 56 * 1024 * 1024

COL_AX, COL_AB, COL_AC, COL_Z = 0, 1024, 2048, 3072
COL_XBC = 4096
COL_Q, COL_K = 6144, 6656
COL_V, COL_G = 7168, 8192
COL_GATES = 9216
BIG_WIDTH = 12288
SMALL_DT_LANE = 0
SMALL_GK_LANE = GLA_GATE_RANK


def _split2(x):
    hi = x.astype(BF16)
    lo = (x - hi.astype(F32)).astype(BF16)
    return hi, lo


def _split3(x):
    hi = x.astype(BF16)
    r = x - hi.astype(F32)
    mid = r.astype(BF16)
    lo = (r - mid.astype(F32)).astype(BF16)
    return hi, mid, lo


def _dot(a, b):
    return jnp.dot(a, b, preferred_element_type=F32)


def _dot_nt(a, b):
    return lax.dot_general(a, b, (((1,), (1,)), ((), ())), preferred_element_type=F32)


def _dot_tn(a, b):
    return lax.dot_general(a, b, (((0,), (0,)), ((), ())), preferred_element_type=F32)


def _dot_exact_rhs(x, m):
    hi, mid, lo = _split3(x)
    return _dot(hi, m) + _dot(mid, m) + _dot(lo, m)


def _dot_exact_lhs(m, x):
    hi, mid, lo = _split3(x)
    return _dot(m, hi) + _dot(m, mid) + _dot(m, lo)


def _expand2(x, e):
    hi, lo = _split2(x)
    return _dot(hi, e) + _dot(lo, e)


def _silu(x):
    return x * jax.nn.sigmoid(x)


def _softplus(x):
    return jnp.maximum(x, 0.0) + jnp.log1p(jnp.exp(-jnp.abs(x)))


def _log_sigmoid(x):
    return jnp.minimum(x, 0.0) - jnp.log1p(jnp.exp(-jnp.abs(x)))


def _shifted_rows(ext, k, width, rows):
    start = HALO_ROWS - (width - 1) + k
    return ext[start:start + rows]


def _inproj_kernel(x_ref, nw_ref, wbig_ref, wsmall_ref, ubig_ref, usmall_ref, h_ref):
    @pl.when(pl.program_id(1) == 0)
    def _():
        x = x_ref[...]
        h = x * lax.rsqrt(jnp.mean(x * x, axis=-1, keepdims=True) + EPS) * nw_ref[...]
        hb = h.astype(BF16)
        h_ref[...] = hb
        usmall_ref[...] = _dot(hb, wsmall_ref[...])

    ubig_ref[...] = _dot(h_ref[...], wbig_ref[...]).astype(BF16)


def _inproj(x2, norm_w, wbig, wsmall, *, tm, tn):
    m = x2.shape[0]
    return pl.pallas_call(
        _inproj_kernel,
        grid=(m // tm, BIG_WIDTH // tn),
        in_specs=[
            pl.BlockSpec((tm, D_MODEL), lambda i, j: (i, 0)),
            pl.BlockSpec((1, D_MODEL), lambda i, j: (0, 0)),
            pl.BlockSpec((D_MODEL, tn), lambda i, j: (0, j)),
            pl.BlockSpec((D_MODEL, LANES), lambda i, j: (0, 0)),
        ],
        out_specs=[
            pl.BlockSpec((tm, tn), lambda i, j: (i, j)),
            pl.BlockSpec((tm, LANES), lambda i, j: (i, 0)),
        ],
        out_shape=[
            jax.ShapeDtypeStruct((m, BIG_WIDTH), BF16),
            jax.ShapeDtypeStruct((m, LANES), F32),
        ],
        scratch_shapes=[pltpu.VMEM((tm, D_MODEL), BF16)],
        compiler_params=pltpu.CompilerParams(
            dimension_semantics=("arbitrary", "arbitrary"),
            vmem_limit_bytes=VMEM_LIMIT_BYTES),
        name="inproj",
    )(x2, norm_w, wbig, wsmall)


def _conv_mixer_kernel(ax_ref, ab_ref, ac_ref, hax_ref, hac_ref, gate_ref, cw_ref, wout_ref,
                       out_ref, *, tiles_per_seq):
    ts = ax_ref.shape[0]
    first = (pl.program_id(0) % tiles_per_seq) == 0
    u = ac_ref[...].astype(F32) * ax_ref[...].astype(F32)
    hu = hac_ref[...].astype(F32) * hax_ref[...].astype(F32)
    hu = jnp.where(first, 0.0, hu)
    ext = jnp.concatenate([hu, u], axis=0)
    cw = cw_ref[...]
    y = _shifted_rows(ext, 0, SC_CONV_WIDTH, ts) * cw[0:1]
    for k in range(1, SC_CONV_WIDTH):
        y = y + _shifted_rows(ext, k, SC_CONV_WIDTH, ts) * cw[k:k + 1]
    v = (ab_ref[...].astype(F32) * y).astype(BF16)
    ya = _dot(v, wout_ref[...])
    out_ref[...] = (jax.nn.sigmoid(gate_ref[...].astype(F32)) * ya).astype(BF16)


def _conv_mixer(ubig, conv_w, wout, *, ts, seq):
    m = ubig.shape[0]
    tiles_per_seq = seq // ts
    hb = ts // HALO_ROWS

    def col(c, w):
        return pl.BlockSpec((ts, w), lambda i: (i, c // w))

    def halo(c, w):
        return pl.BlockSpec((HALO_ROWS, w), lambda i: (jnp.maximum(i * hb - 1, 0), c // w))

    return pl.pallas_call(
        functools.partial(_conv_mixer_kernel, tiles_per_seq=tiles_per_seq),
        grid=(m // ts,),
        in_specs=[
            col(COL_AX, SC_WIDTH), col(COL_AB, SC_WIDTH), col(COL_AC, SC_WIDTH),
            halo(COL_AX, SC_WIDTH), halo(COL_AC, SC_WIDTH),
            col(COL_GATES, D_MODEL),
            pl.BlockSpec((SC_CONV_WIDTH, SC_WIDTH), lambda i: (0, 0)),
            pl.BlockSpec((SC_WIDTH, D_MODEL), lambda i: (0, 0)),
        ],
        out_specs=pl.BlockSpec((ts, D_MODEL), lambda i: (i, 0)),
        out_shape=jax.ShapeDtypeStruct((m, D_MODEL), BF16),
        compiler_params=pltpu.CompilerParams(
            dimension_semantics=("arbitrary",),
            vmem_limit_bytes=VMEM_LIMIT_BYTES),
        name="conv_mixer",
    )(ubig, ubig, ubig, ubig, ubig, ubig, conv_w, wout)


def _ssd_kernel(z_ref, xbc_ref, hxbc_ref, small_ref, gate_ref, cw_ref, cb_ref, dtb_ref, alog_ref,
                drow_ref, nw_ref, wout_ref, out_ref, ext_ref, state_ref, y_ref):
    ts = z_ref.shape[0]
    q = SSM_CHUNK
    n_chunks = ts // q
    gw = SSM_GROUP_WIDTH
    ns = SSM_STATE

    j = pl.program_id(1)

    @pl.when(j == 0)
    def _():
        state_ref[...] = jnp.zeros_like(state_ref)

    ext_ref[0:HALO_ROWS, :] = jnp.where(j == 0, jnp.zeros_like(hxbc_ref[...]), hxbc_ref[...])
    ext_ref[HALO_ROWS:, :] = xbc_ref[...]

    row = lax.broadcasted_iota(jnp.int32, (q, q), 0)
    colm = lax.broadcasted_iota(jnp.int32, (q, q), 1)
    causal = row >= colm
    ltri = jnp.where(causal, 1.0, 0.0).astype(BF16)
    utri = jnp.where(row <= colm, 1.0, 0.0).astype(BF16)
    hrow = lax.broadcasted_iota(jnp.int32, (LANES, SSM_D_INNER), 0)
    hcol = lax.broadcasted_iota(jnp.int32, (LANES, SSM_D_INNER), 1)
    expand = jnp.where(hcol // SSM_HEADDIM == hrow, 1.0, 0.0).astype(BF16)
    lane = lax.broadcasted_iota(jnp.int32, (1, LANES), 1)
    head_lane = lane < SSM_HEADS
    a_full = jnp.where(head_lane, -jnp.exp(alog_ref[...]), 0.0)
    glane = lax.broadcasted_iota(jnp.int32, (1, gw), 1)

    cw = cw_ref[...]
    cbias = cb_ref[...]
    dtb = dtb_ref[...]
    drow = drow_ref[...]
    nw = nw_ref[...]

    def chunk(c, carry):
        r0 = pl.multiple_of(c * q, q)
        ext = ext_ref[pl.ds(r0, q + HALO_ROWS), :].astype(F32)
        pre = _shifted_rows(ext, 0, SSM_CONV_WIDTH, q) * cw[0:1]
        for k in range(1, SSM_CONV_WIDTH):
            pre = pre + _shifted_rows(ext, k, SSM_CONV_WIDTH, q) * cw[k:k + 1]
        xbc = _silu(pre + cbias)
        xs = xbc[:, :SSM_D_INNER]
        bm = xbc[:, SSM_D_INNER:SSM_D_INNER + SSM_GROUPS * ns]
        cm = xbc[:, SSM_D_INNER + SSM_GROUPS * ns:]

        dt = _softplus(small_ref[pl.ds(r0, q), :] + dtb)
        dta = dt * a_full
        acol = _dot_exact_lhs(ltri, dta)
        arow = _dot_exact_rhs(dta.T, utri)
        alast = acol[q - 1:q, :]
        dt_x = _expand2(dt, expand)
        ea_x = _expand2(jnp.exp(acol), expand)
        dte_x = _expand2(jnp.exp(alast - acol), expand)
        cdec_x = ea_x[q - 1:q, :]
        xdt = xs * dt_x
        xdt_b = xdt.astype(BF16)
        xdte_b = (xdt * dte_x).astype(BF16)

        for g in range(SSM_GROUPS):
            cm_g = cm[:, g * ns:(g + 1) * ns].astype(BF16)
            bm_g32 = bm[:, g * ns:(g + 1) * ns]
            bm_g = bm_g32.astype(BF16)
            cbm = _dot_nt(cm_g, bm_g)
            xg = xdt_b[:, g * gw:(g + 1) * gw]
            lhs, rhs = [], []
            for r in range(SSM_HPG):
                h = g * SSM_HPG + r
                seg = acol[:, h:h + 1] - arow[h:h + 1, :]
                dec = jnp.where(causal, jnp.exp(jnp.where(causal, seg, 0.0)), 0.0)
                lhs.append((cbm * dec).astype(BF16))
                rhs.append(jnp.where(glane // SSM_HEADDIM == r, xg, jnp.zeros_like(xg)))
            y_diag = _dot(jnp.concatenate(lhs, axis=1), jnp.concatenate(rhs, axis=0))
            st = state_ref[g]
            y_off = _dot(cm_g, st.astype(BF16)) * ea_x[:, g * gw:(g + 1) * gw]
            state_ref[g] = (st * cdec_x[:, g * gw:(g + 1) * gw]
                            + _dot(bm_g32.T.astype(BF16), xdte_b[:, g * gw:(g + 1) * gw]))
            yg = y_diag + y_off + xs[:, g * gw:(g + 1) * gw] * drow[:, g * gw:(g + 1) * gw]
            yg = yg * _silu(z_ref[pl.ds(r0, q), g * gw:(g + 1) * gw].astype(F32))
            yg = yg * lax.rsqrt(jnp.mean(yg * yg, axis=-1, keepdims=True) + EPS)
            y_ref[pl.ds(r0, q), g * gw:(g + 1) * gw] = (yg * nw[:, g * gw:(g + 1) * gw]).astype(BF16)
        return carry

    lax.fori_loop(0, n_chunks, chunk, 0)
    yb = _dot(y_ref[...], wout_ref[...])
    out_ref[...] = (jax.nn.sigmoid(gate_ref[...].astype(F32)) * yb).astype(BF16)


def _ssd_mixer(ubig, usmall, conv_w, conv_b, dt_bias, a_log, d_row, norm_w, wout, *, ts, batch, seq):
    m = ubig.shape[0]
    tps = seq // ts
    hb = ts // HALO_ROWS

    def rows(w, c):
        return pl.BlockSpec((ts, w), lambda b, j: (b * tps + j, c // w))

    def const(shape):
        return pl.BlockSpec(shape, lambda b, j: (0,) * len(shape))

    return pl.pallas_call(
        _ssd_kernel,
        grid=(batch, tps),
        in_specs=[
            rows(SSM_D_INNER, COL_Z),
            rows(SSM_CONV_DIM, COL_XBC),
            pl.BlockSpec((HALO_ROWS, SSM_CONV_DIM),
                         lambda b, j: (jnp.maximum((b * tps + j) * hb - 1, 0), COL_XBC // SSM_CONV_DIM)),
            pl.BlockSpec((ts, LANES), lambda b, j: (b * tps + j, 0)),
            rows(D_MODEL, COL_GATES + D_MODEL),
            const((SSM_CONV_WIDTH, SSM_CONV_DIM)),
            const((1, SSM_CONV_DIM)),
            const((1, LANES)),
            const((1, LANES)),
            const((1, SSM_D_INNER)),
            const((1, SSM_D_INNER)),
            const((SSM_D_INNER, D_MODEL)),
        ],
        out_specs=pl.BlockSpec((ts, D_MODEL), lambda b, j: (b * tps + j, 0)),
        out_shape=jax.ShapeDtypeStruct((m, D_MODEL), BF16),
        scratch_shapes=[
            pltpu.VMEM((ts + HALO_ROWS, SSM_CONV_DIM), BF16),
            pltpu.VMEM((SSM_GROUPS, SSM_STATE, SSM_GROUP_WIDTH), F32),
            pltpu.VMEM((ts, SSM_D_INNER), BF16),
        ],
        compiler_params=pltpu.CompilerParams(
            dimension_semantics=("arbitrary", "arbitrary"),
            vmem_limit_bytes=VMEM_LIMIT_BYTES),
        name="ssd_mixer",
    )(ubig, ubig, ubig, usmall, ubig, conv_w, conv_b, dt_bias, a_log, d_row, norm_w, wout)


def _gla_kernel(q_ref, k_ref, v_ref, g_ref, small_ref, gate_ref, wgk_ref, bgk_ref, nw_ref, wout_ref,
                out_ref, state_ref, o_ref):
    ts = q_ref.shape[0]
    t = GLA_CHUNK
    n_chunks = ts // t
    dk, dv = GLA_HEAD_K, GLA_HEAD_V

    @pl.when(pl.program_id(1) == 0)
    def _():
        state_ref[...] = jnp.zeros_like(state_ref)

    row = lax.broadcasted_iota(jnp.int32, (t, t), 0)
    colm = lax.broadcasted_iota(jnp.int32, (t, t), 1)
    causal = row >= colm
    ltri = jnp.where(causal, 1.0, 0.0).astype(BF16)
    w_hi, w_lo = _split2(wgk_ref[...])
    bgk = bgk_ref[...]
    nw = nw_ref[...]
    scale = dk ** -0.5

    def chunk(c, carry):
        r0 = pl.multiple_of(c * t, t)
        s_hi, s_lo = _split2(small_ref[pl.ds(r0, t), :])
        zg = _dot(s_hi, w_hi) + _dot(s_hi, w_lo) + _dot(s_lo, w_hi) + bgk
        gk = _log_sigmoid(zg) / GLA_GATE_NORMALIZER
        gcum = _dot_exact_lhs(ltri, gk)
        glast = gcum[t - 1:t, :]
        qf = q_ref[pl.ds(r0, t), :].astype(F32)
        kf = k_ref[pl.ds(r0, t), :].astype(F32)
        q_in = (qf * scale * jnp.exp(gcum)).astype(BF16)
        k_in = (kf * jnp.exp(-gcum)).astype(BF16)
        k_end = (kf * jnp.exp(glast - gcum)).astype(BF16)
        cdec = jnp.exp(glast)
        for h in range(GLA_HEADS):
            qh = q_in[:, h * dk:(h + 1) * dk]
            kh = k_in[:, h * dk:(h + 1) * dk]
            keh = k_end[:, h * dk:(h + 1) * dk]
            vh = v_ref[pl.ds(r0, t), h * dv:(h + 1) * dv]
            sc = jnp.where(causal, _dot_nt(qh, kh), 0.0)
            st = state_ref[h]
            o = _dot(sc.astype(BF16), vh) + _dot_nt(qh, st.astype(BF16))
            state_ref[h] = st * cdec[:, h * dk:(h + 1) * dk] + _dot_tn(vh, keh)
            o = o * lax.rsqrt(jnp.mean(o * o, axis=-1, keepdims=True) + EPS) * nw
            o = o * _silu(g_ref[pl.ds(r0, t), h * dv:(h + 1) * dv].astype(F32))
            o_ref[pl.ds(r0, t), h * dv:(h + 1) * dv] = o.astype(BF16)
        return carry

    lax.fori_loop(0, n_chunks, chunk, 0)
    yc = _dot(o_ref[...], wout_ref[...])
    out_ref[...] = (jax.nn.sigmoid(gate_ref[...].astype(F32)) * yc).astype(BF16)


def _gla_mixer(ubig, usmall, wgk_full, b_gk, norm_w, wout, *, ts, batch, seq):
    m = ubig.shape[0]
    tps = seq // ts

    def rows(w, c):
        return pl.BlockSpec((ts, w), lambda b, j: (b * tps + j, c // w))

    def const(shape):
        return pl.BlockSpec(shape, lambda b, j: (0,) * len(shape))

    return pl.pallas_call(
        _gla_kernel,
        grid=(batch, tps),
        in_specs=[
            rows(GLA_KEY_DIM, COL_Q),
            rows(GLA_KEY_DIM, COL_K),
            rows(GLA_VAL_DIM, COL_V),
            rows(GLA_VAL_DIM, COL_G),
            pl.BlockSpec((ts, LANES), lambda b, j: (b * tps + j, 0)),
            rows(D_MODEL, COL_GATES + 2 * D_MODEL),
            const((LANES, GLA_KEY_DIM)),
            const((1, GLA_KEY_DIM)),
            const((1, GLA_HEAD_V)),
            const((GLA_VAL_DIM, D_MODEL)),
        ],
        out_specs=pl.BlockSpec((ts, D_MODEL), lambda b, j: (b * tps + j, 0)),
        out_shape=jax.ShapeDtypeStruct((m, D_MODEL), BF16),
        scratch_shapes=[
            pltpu.VMEM((GLA_HEADS, GLA_HEAD_V, GLA_HEAD_K), F32),
            pltpu.VMEM((ts, GLA_VAL_DIM), BF16),
        ],
        compiler_params=pltpu.CompilerParams(
            dimension_semantics=("arbitrary", "arbitrary"),
            vmem_limit_bytes=VMEM_LIMIT_BYTES),
        name="gla_mixer",
    )(ubig, ubig, ubig, ubig, usmall, ubig, wgk_full, b_gk, norm_w, wout)


def _merge_mlp_kernel(x_ref, ya_ref, yb_ref, yc_ref, wo_ref, nw_ref, wup_ref, wdown_ref, nf_ref,
                      out_ref, *, ff_chunk, final_norm):
    merged = ya_ref[...].astype(F32) + yb_ref[...].astype(F32) + yc_ref[...].astype(F32)
    x = x_ref[...] + _dot(merged.astype(BF16), wo_ref[...])
    h = x * lax.rsqrt(jnp.mean(x * x, axis=-1, keepdims=True) + EPS) * nw_ref[...]
    hb = h.astype(BF16)
    acc = x
    for c in range(D_FF // ff_chunk):
        up = _dot(hb, wup_ref[:, c * ff_chunk:(c + 1) * ff_chunk])
        act = jnp.square(jnp.maximum(up, 0.0)).astype(BF16)
        acc = acc + _dot(act, wdown_ref[c * ff_chunk:(c + 1) * ff_chunk, :])
    if final_norm:
        acc = acc * lax.rsqrt(jnp.mean(acc * acc, axis=-1, keepdims=True) + EPS) * nf_ref[...]
    out_ref[...] = acc


def _merge_mlp(x2, ya, yb, yc, wo, norm_w, wup, wdown, norm_f, *, tm, final_norm):
    m = x2.shape[0]

    def rows():
        return pl.BlockSpec((tm, D_MODEL), lambda i: (i, 0))

    def const(shape):
        return pl.BlockSpec(shape, lambda i: (0, 0), pipeline_mode=pl.Buffered(1))

    return pl.pallas_call(
        functools.partial(_merge_mlp_kernel, ff_chunk=1024, final_norm=final_norm),
        grid=(m // tm,),
        in_specs=[
            rows(), rows(), rows(), rows(),
            const((D_MODEL, D_MODEL)),
            const((1, D_MODEL)),
            const((D_MODEL, D_FF)),
            const((D_FF, D_MODEL)),
            const((1, D_MODEL)),
        ],
        out_specs=rows(),
        out_shape=jax.ShapeDtypeStruct((m, D_MODEL), F32),
        compiler_params=pltpu.CompilerParams(
            dimension_semantics=("arbitrary",),
            vmem_limit_bytes=VMEM_LIMIT_BYTES),
        name="merge_mlp",
    )(x2, ya, yb, yc, wo, norm_w, wup, wdown, norm_f)


def _pick(n, pref):
    t = min(pref, n)
    while n % t:
        t //= 2
    return t


def _reorder_w_in(w):
    o_dt = 3 * SC_WIDTH + SSM_D_INNER + SSM_CONV_DIM
    o_q = o_dt + SSM_HEADS
    o_gk = o_q + 2 * GLA_KEY_DIM + 2 * GLA_VAL_DIM
    o_gates = o_gk + GLA_GATE_RANK
    big = jnp.concatenate([w[:, :o_dt], w[:, o_q:o_gk], w[:, o_gates:]], axis=1)
    pad = jnp.zeros((w.shape[0], LANES - SSM_HEADS - GLA_GATE_RANK), w.dtype)
    small = jnp.concatenate([w[:, o_dt:o_q], w[:, o_gk:o_gates], pad], axis=1)
    return big.astype(BF16), small.astype(BF16)


def _pad_lanes(v, offset):
    out = jnp.zeros((1, LANES), F32)
    return out.at[0, offset:offset + v.shape[0]].set(v.astype(F32))


def kernel(x, norm_mix_w, w_in, conv_a_w, w_out_a, ssm_conv_w, ssm_conv_b, ssm_dt_bias, ssm_a_log, ssm_d,
           ssm_norm_w, w_out_ssm, gla_w_gk2, gla_b_gk, gla_norm_w, w_out_gla, w_o, norm_mlp_w, w_mlp_up,
           w_mlp_down, norm_f_w):
    b, s, d = x.shape
    m = b * s
    depth = w_in.shape[0]
    x2 = x.reshape(m, d).astype(F32)
    tm_proj = _pick(m, 1024)
    ts_conv = _pick(s, 512)
    ts_ssd = _pick(s, 512)
    ts_gla = _pick(s, 512)
    tm_mlp = _pick(m, 512)
    row = lambda v: v.reshape(1, -1).astype(F32)

    for i in range(depth):
        wbig, wsmall = _reorder_w_in(w_in[i])
        ubig, usmall = _inproj(x2, row(norm_mix_w[i]), wbig, wsmall, tm=tm_proj, tn=1024)
        ya = _conv_mixer(ubig, conv_a_w[i].astype(F32), w_out_a[i].astype(BF16), ts=ts_conv, seq=s)
        yb = _ssd_mixer(
            ubig, usmall, ssm_conv_w[i].astype(F32), row(ssm_conv_b[i]),
            _pad_lanes(ssm_dt_bias[i], SMALL_DT_LANE), _pad_lanes(ssm_a_log[i], SMALL_DT_LANE),
            row(jnp.repeat(ssm_d[i], SSM_HEADDIM)), row(ssm_norm_w[i]), w_out_ssm[i].astype(BF16),
            ts=ts_ssd, batch=b, seq=s)
        wgk_full = jnp.zeros((LANES, GLA_KEY_DIM), F32).at[
            SMALL_GK_LANE:SMALL_GK_LANE + GLA_GATE_RANK].set(gla_w_gk2[i].astype(F32))
        yc = _gla_mixer(ubig, usmall, wgk_full, row(gla_b_gk[i]), row(gla_norm_w[i]),
                        w_out_gla[i].astype(BF16), ts=ts_gla, batch=b, seq=s)
        x2 = _merge_mlp(x2, ya, yb, yc, w_o[i].astype(BF16), row(norm_mlp_w[i]),
                        w_mlp_up[i].astype(BF16), w_mlp_down[i].astype(BF16), row(norm_f_w),
                        tm=tm_mlp, final_norm=(i == depth - 1))
    return x2.reshape(b, s, d)
```

```python
import functools

import jax
import jax.numpy as jnp
from jax import lax
from jax.experimental import pallas as pl
from jax.experimental.pallas import tpu as pltpu

F32 = jnp.float32
BF16 = jnp.bfloat16

EPS = 1e-6
D_MODEL = 1024
SC_WIDTH = 1024
SC_CONV_WIDTH = 3
SSM_D_INNER = 1024
SSM_HEADDIM = 64
SSM_HEADS = SSM_D_INNER // SSM_HEADDIM
SSM_GROUPS = 4
SSM_HPG = SSM_HEADS // SSM_GROUPS
SSM_STATE = 128
SSM_CONV_WIDTH = 4
SSM_CHUNK = 128
SSM_CONV_DIM = SSM_D_INNER + 2 * SSM_GROUPS * SSM_STATE
SSM_GROUP_WIDTH = SSM_D_INNER // SSM_GROUPS
GLA_HEADS = 4
GLA_KEY_DIM = D_MODEL // 2
GLA_VAL_DIM = D_MODEL
GLA_HEAD_K = GLA_KEY_DIM // GLA_HEADS
GLA_HEAD_V = GLA_VAL_DIM // GLA_HEADS
GLA_GATE_RANK = 16
GLA_GATE_NORMALIZER = 16.0
GLA_CHUNK = 64
N_BRANCHES = 3
D_FF = 4 * D_MODEL

LANES = 128
SUBLANES = 8
VMEM_LIMIT_BYTES = 56 * 1024 * 1024

PROJ_BLOCK = 1024
PB_AX, PB_AB, PB_AC, PB_Z, PB_XBC, PB_QK, PB_V, PB_G, PB_GATES = 0, 1, 2, 3, 4, 6, 7, 8, 9
N_PROJ_BLOCKS = 12
U_VA, U_Z, U_XBC, U_Q, U_K, U_V, U_G, U_GATES = 0, 1024, 2048, 4096, 4608, 5120, 6144, 7168
U_WIDTH = 10240
SMALL_DT_LANE = 0
SMALL_GK_LANE = GLA_GATE_RANK


def _split2(x):
    hi = x.astype(BF16)
    lo = (x - hi.astype(F32)).astype(BF16)
    return hi, lo


def _split3(x):
    hi = x.astype(BF16)
    r = x - hi.astype(F32)
    mid = r.astype(BF16)
    lo = (r - mid.astype(F32)).astype(BF16)
    return hi, mid, lo


def _dot(a, b):
    return jnp.dot(a, b, preferred_element_type=F32)


def _dot_nt(a, b):
    return lax.dot_general(a, b, (((1,), (1,)), ((), ())), preferred_element_type=F32)


def _dot_tn(a, b):
    return lax.dot_general(a, b, (((0,), (0,)), ((), ())), preferred_element_type=F32)


def _dot_exact_rhs(x, m):
    hi, mid, lo = _split3(x)
    return _dot(hi, m) + _dot(mid, m) + _dot(lo, m)


def _dot_exact_lhs(m, x):
    hi, mid, lo = _split3(x)
    return _dot(m, hi) + _dot(m, mid) + _dot(m, lo)


def _expand2(x, e):
    hi, lo = _split2(x)
    return _dot(hi, e) + _dot(lo, e)


def _silu(x):
    return x * jax.nn.sigmoid(x)


def _softplus(x):
    return jnp.maximum(x, 0.0) + jnp.log1p(jnp.exp(-jnp.abs(x)))


def _log_sigmoid(x):
    return jnp.minimum(x, 0.0) - jnp.log1p(jnp.exp(-jnp.abs(x)))


def _rms(x):
    return x * lax.rsqrt(jnp.mean(x * x, axis=-1, keepdims=True) + EPS)


def _proj_kernel(x_ref, nw_ref, w_ref, wsmall_ref, cwa_ref, cwb_ref, cbb_ref, u_ref, usmall_ref,
                 work_ref, h_ref, *, tiles_per_seq):
    tm = x_ref.shape[0]
    first = (pl.program_id(0) % tiles_per_seq) == 0
    n_conv, n_slabs = work_ref.shape[0], work_ref.shape[1]

    @pl.when(first)
    def _():
        work_ref[:, :, 0:SUBLANES, :] = jnp.zeros((n_conv, n_slabs, SUBLANES, LANES), F32)

    @pl.when(jnp.logical_not(first))
    def _():
        work_ref[:, :, 0:SUBLANES, :] = work_ref[:, :, tm:tm + SUBLANES, :]

    h_ref[...] = (_rms(x_ref[...]) * nw_ref[...]).astype(BF16)
    usmall_ref[...] = _dot(h_ref[...], wsmall_ref[...])

    def proj(blk):
        return _dot(h_ref[...], w_ref[:, blk * PROJ_BLOCK:(blk + 1) * PROJ_BLOCK])

    def causal_conv(slot, pre, cw, width):
        for sl in range(n_slabs):
            work_ref[slot, sl, SUBLANES:SUBLANES + tm, :] = pre[:, sl * LANES:(sl + 1) * LANES]
        y = None
        for k in range(width):
            start = SUBLANES - (width - 1) + k
            shifted = jnp.concatenate(
                [work_ref[slot, sl, pl.ds(start, tm, stride=1), :] for sl in range(n_slabs)], axis=1)
            term = shifted * cw[k:k + 1]
            y = term if y is None else y + term
        return y

    def put(col, val):
        u_ref[:, col:col + PROJ_BLOCK] = val.astype(BF16)

    ax = proj(PB_AX)
    ac = proj(PB_AC)
    conv_a = causal_conv(0, ac * ax, cwa_ref[...], SC_CONV_WIDTH)
    put(U_VA, proj(PB_AB) * conv_a)
    put(U_Z, _silu(proj(PB_Z)))
    cwb = cwb_ref[...]
    cbb = cbb_ref[...]
    for c in range(SSM_CONV_DIM // PROJ_BLOCK):
        cols = slice(c * PROJ_BLOCK, (c + 1) * PROJ_BLOCK)
        y = causal_conv(1 + c, proj(PB_XBC + c), cwb[:, cols], SSM_CONV_WIDTH)
        put(U_XBC + c * PROJ_BLOCK, _silu(y + cbb[:, cols]))
    put(U_Q, proj(PB_QK))
    put(U_V, proj(PB_V))
    put(U_G, _silu(proj(PB_G)))
    for c in range(N_BRANCHES):
        put(U_GATES + c * PROJ_BLOCK, jax.nn.sigmoid(proj(PB_GATES + c)))


def _resident(shape):
    return pl.BlockSpec(shape, lambda *_: (0,) * len(shape), pipeline_mode=pl.Buffered(1))


def _proj(x2, norm_w, wbig, wsmall, conv_a_w, conv_b_w, conv_b_b, *, tm, seq):
    m = x2.shape[0]
    n_conv = 1 + SSM_CONV_DIM // PROJ_BLOCK
    return pl.pallas_call(
        functools.partial(_proj_kernel, tiles_per_seq=seq // tm),
        grid=(m // tm,),
        in_specs=[
            pl.BlockSpec((tm, D_MODEL), lambda i: (i, 0)),
            _resident((1, D_MODEL)),
            _resident((D_MODEL, N_PROJ_BLOCKS * PROJ_BLOCK)),
            _resident((D_MODEL, LANES)),
            _resident((SC_CONV_WIDTH, SC_WIDTH)),
            _resident((SSM_CONV_WIDTH, SSM_CONV_DIM)),
            _resident((1, SSM_CONV_DIM)),
        ],
        out_specs=[
            pl.BlockSpec((tm, U_WIDTH), lambda i: (i, 0)),
            pl.BlockSpec((tm, LANES), lambda i: (i, 0)),
        ],
        out_shape=[
            jax.ShapeDtypeStruct((m, U_WIDTH), BF16),
            jax.ShapeDtypeStruct((m, LANES), F32),
        ],
        scratch_shapes=[pltpu.VMEM((n_conv, PROJ_BLOCK // LANES, tm + SUBLANES, LANES), F32),
                        pltpu.VMEM((tm, D_MODEL), BF16)],
        compiler_params=pltpu.CompilerParams(
            dimension_semantics=("arbitrary",),
            vmem_limit_bytes=VMEM_LIMIT_BYTES),
        name="proj",
    )(x2, norm_w, wbig, wsmall, conv_a_w, conv_b_w, conv_b_b)


def _ssd_chunks(xbc_ref, zs_ref, small_ref, dtb_ref, alog_ref, drow_ref, nw_ref, state_ref, y_ref):
    ts = xbc_ref.shape[0]
    q = SSM_CHUNK
    gw = SSM_GROUP_WIDTH
    ns = SSM_STATE

    row = lax.broadcasted_iota(jnp.int32, (q, q), 0)
    colm = lax.broadcasted_iota(jnp.int32, (q, q), 1)
    causal = row >= colm
    ltri = jnp.where(causal, 1.0, 0.0).astype(BF16)
    utri = jnp.where(row <= colm, 1.0, 0.0).astype(BF16)
    hrow = lax.broadcasted_iota(jnp.int32, (LANES, SSM_D_INNER), 0)
    hcol = lax.broadcasted_iota(jnp.int32, (LANES, SSM_D_INNER), 1)
    expand = jnp.where(hcol // SSM_HEADDIM == hrow, 1.0, 0.0).astype(BF16)
    lane = lax.broadcasted_iota(jnp.int32, (1, LANES), 1)
    a_full = jnp.where(lane < SSM_HEADS, -jnp.exp(alog_ref[...]), 0.0)
    glane = lax.broadcasted_iota(jnp.int32, (1, gw), 1)
    dtb = dtb_ref[...]
    drow = drow_ref[...]
    nw = nw_ref[...]

    def chunk(c, carry):
        r0 = pl.multiple_of(c * q, q)
        xs_b = xbc_ref[pl.ds(r0, q), 0:SSM_D_INNER]
        xs = xs_b.astype(F32)
        dt = _softplus(small_ref[pl.ds(r0, q), :] + dtb)
        dta = dt * a_full
        acol = _dot_exact_lhs(ltri, dta)
        arow = _dot_exact_rhs(dta.T, utri)
        alast = acol[q - 1:q, :]
        dt_x = _expand2(dt, expand)
        ea_x = _expand2(jnp.exp(acol), expand)
        dte_x = _expand2(jnp.exp(alast - acol), expand)
        cdec_x = ea_x[q - 1:q, :]
        xdt = xs * dt_x
        xdt_b = xdt.astype(BF16)
        xdte_b = (xdt * dte_x).astype(BF16)

        for g in range(SSM_GROUPS):
            gcols = slice(g * gw, (g + 1) * gw)
            bm_g = xbc_ref[pl.ds(r0, q), SSM_D_INNER + g * ns:SSM_D_INNER + (g + 1) * ns]
            cm_g = xbc_ref[pl.ds(r0, q), SSM_D_INNER + (SSM_GROUPS + g) * ns:
                           SSM_D_INNER + (SSM_GROUPS + g + 1) * ns]
            cbm = _dot_nt(cm_g, bm_g)
            xg = xdt_b[:, gcols]
            lhs, rhs = [], []
            for r in range(SSM_HPG):
                h = g * SSM_HPG + r
                seg = acol[:, h:h + 1] - arow[h:h + 1, :]
                dec = jnp.where(causal, jnp.exp(jnp.where(causal, seg, 0.0)), 0.0)
                lhs.append((cbm * dec).astype(BF16))
                rhs.append(jnp.where(glane // SSM_HEADDIM == r, xg, jnp.zeros_like(xg)))
            y_diag = _dot(jnp.concatenate(lhs, axis=1), jnp.concatenate(rhs, axis=0))
            st = state_ref[g]
            y_off = _dot(cm_g, st.astype(BF16)) * ea_x[:, gcols]
            state_ref[g] = st * cdec_x[:, gcols] + _dot_tn(bm_g, xdte_b[:, gcols])
            yg = y_diag + y_off + xs[:, gcols] * drow[:, gcols]
            yg = _rms(yg * zs_ref[pl.ds(r0, q), gcols].astype(F32))
            y_ref[pl.ds(r0, q), gcols] = (yg * nw[:, gcols]).astype(BF16)
        return carry

    lax.fori_loop(0, ts // q, chunk, 0)


def _gla_chunks(q_ref, k_ref, v_ref, gs_ref, small_ref, wgk_ref, bgk_ref, nw_ref, state_ref, o_ref):
    ts = q_ref.shape[0]
    t = GLA_CHUNK
    dk, dv = GLA_HEAD_K, GLA_HEAD_V

    row = lax.broadcasted_iota(jnp.int32, (t, t), 0)
    colm = lax.broadcasted_iota(jnp.int32, (t, t), 1)
    causal = row >= colm
    ltri = jnp.where(causal, 1.0, 0.0).astype(BF16)
    w_hi, w_lo = _split2(wgk_ref[...])
    bgk = bgk_ref[...]
    nw = nw_ref[...]
    scale = dk ** -0.5

    s_hi, s_lo = _split2(small_ref[...])
    zg = _dot(s_hi, w_hi) + _dot(s_hi, w_lo) + _dot(s_lo, w_hi) + bgk
    gk = _log_sigmoid(zg) / GLA_GATE_NORMALIZER

    states = [state_ref[h] for h in range(GLA_HEADS)]
    for c in range(ts // t):
        rows = slice(c * t, (c + 1) * t)
        gcum = _dot_exact_lhs(ltri, gk[rows])
        glast = gcum[t - 1:t, :]
        qf = q_ref[rows, :].astype(F32)
        kf = k_ref[rows, :].astype(F32)
        q_in = (qf * scale * jnp.exp(gcum)).astype(BF16)
        k_in = (kf * jnp.exp(-gcum)).astype(BF16)
        k_end = (kf * jnp.exp(glast - gcum)).astype(BF16)
        cdec = jnp.exp(glast)
        for h in range(GLA_HEADS):
            kcols = slice(h * dk, (h + 1) * dk)
            vcols = slice(h * dv, (h + 1) * dv)
            qh = q_in[:, kcols]
            vh = v_ref[rows, vcols]
            sc = jnp.where(causal, _dot_nt(qh, k_in[:, kcols]), 0.0)
            st = states[h]
            o = _dot(sc.astype(BF16), vh) + _dot_nt(qh, st.astype(BF16))
            states[h] = st * cdec[:, kcols] + _dot_tn(vh, k_end[:, kcols])
            o = _rms(o) * nw * gs_ref[rows, vcols].astype(F32)
            o_ref[rows, vcols] = o.astype(BF16)
    for h in range(GLA_HEADS):
        state_ref[h] = states[h]


def _mixer_kernel(x_ref, va_ref, zs_ref, xbc_ref, q_ref, k_ref, v_ref, gs_ref, ga_ref, gb_ref, gc_ref,
                  small_ref, dtb_ref, alog_ref, drow_ref, snw_ref, wgk_ref, bgk_ref, gnw_ref,
                  wa_ref, wb_ref, wc_ref, wo_ref, out_ref, sstate_ref, gstate_ref, y_ref, o_ref):
    @pl.when(pl.program_id(1) == 0)
    def _():
        sstate_ref[...] = jnp.zeros_like(sstate_ref)
        gstate_ref[...] = jnp.zeros_like(gstate_ref)

    _ssd_chunks(xbc_ref, zs_ref, small_ref, dtb_ref, alog_ref, drow_ref, snw_ref, sstate_ref, y_ref)
    _gla_chunks(q_ref, k_ref, v_ref, gs_ref, small_ref, wgk_ref, bgk_ref, gnw_ref, gstate_ref, o_ref)

    merged = (ga_ref[...].astype(F32) * _dot(va_ref[...], wa_ref[...])
              + gb_ref[...].astype(F32) * _dot(y_ref[...], wb_ref[...])
              + gc_ref[...].astype(F32) * _dot(o_ref[...], wc_ref[...]))
    out_ref[...] = x_ref[...] + _dot(merged.astype(BF16), wo_ref[...])


def _mixer(x2, u, usmall, dt_bias, a_log, d_row, ssm_nw, wgk_full, b_gk, gla_nw, wa, wb, wc, wo,
           *, ts, batch, seq):
    m = x2.shape[0]
    tps = seq // ts

    def rows(w, c):
        return pl.BlockSpec((ts, w), lambda b, j: (b * tps + j, c // w))

    return pl.pallas_call(
        _mixer_kernel,
        grid=(batch, tps),
        in_specs=[
            rows(D_MODEL, 0),
            rows(SC_WIDTH, U_VA),
            rows(SSM_D_INNER, U_Z),
            rows(SSM_CONV_DIM, U_XBC),
            rows(GLA_KEY_DIM, U_Q),
            rows(GLA_KEY_DIM, U_K),
            rows(GLA_VAL_DIM, U_V),
            rows(GLA_VAL_DIM, U_G),
            rows(D_MODEL, U_GATES),
            rows(D_MODEL, U_GATES + D_MODEL),
            rows(D_MODEL, U_GATES + 2 * D_MODEL),
            rows(LANES, 0),
            _resident((1, LANES)),
            _resident((1, LANES)),
            _resident((1, SSM_D_INNER)),
            _resident((1, SSM_D_INNER)),
            _resident((LANES, GLA_KEY_DIM)),
            _resident((1, GLA_KEY_DIM)),
            _resident((1, GLA_HEAD_V)),
            _resident((SC_WIDTH, D_MODEL)),
            _resident((SSM_D_INNER, D_MODEL)),
            _resident((GLA_VAL_DIM, D_MODEL)),
            _resident((D_MODEL, D_MODEL)),
        ],
        out_specs=pl.BlockSpec((ts, D_MODEL), lambda b, j: (b * tps + j, 0)),
        out_shape=jax.ShapeDtypeStruct((m, D_MODEL), F32),
        scratch_shapes=[
            pltpu.VMEM((SSM_GROUPS, SSM_STATE, SSM_GROUP_WIDTH), F32),
            pltpu.VMEM((GLA_HEADS, GLA_HEAD_V, GLA_HEAD_K), F32),
            pltpu.VMEM((ts, SSM_D_INNER), BF16),
            pltpu.VMEM((ts, GLA_VAL_DIM), BF16),
        ],
        compiler_params=pltpu.CompilerParams(
            dimension_semantics=("arbitrary", "arbitrary"),
            vmem_limit_bytes=VMEM_LIMIT_BYTES),
        name="mixer",
    )(x2, u, u, u, u, u, u, u, u, u, u, usmall, dt_bias, a_log, d_row, ssm_nw, wgk_full, b_gk, gla_nw,
      wa, wb, wc, wo)


def _mlp_kernel(x_ref, nw_ref, wup_ref, wdown_ref, nf_ref, out_ref, *, ff_chunk, final_norm):
    x = x_ref[...]
    hb = (_rms(x) * nw_ref[...]).astype(BF16)
    acc = x
    for c in range(D_FF // ff_chunk):
        up = _dot(hb, wup_ref[:, c * ff_chunk:(c + 1) * ff_chunk])
        act = jnp.square(jnp.maximum(up, 0.0)).astype(BF16)
        acc = acc + _dot(act, wdown_ref[c * ff_chunk:(c + 1) * ff_chunk, :])
    if final_norm:
        acc = _rms(acc) * nf_ref[...]
    out_ref[...] = acc


def _mlp(x2, norm_w, wup, wdown, norm_f, *, tm, final_norm):
    m = x2.shape[0]
    return pl.pallas_call(
        functools.partial(_mlp_kernel, ff_chunk=1024, final_norm=final_norm),
        grid=(m // tm,),
        in_specs=[
            pl.BlockSpec((tm, D_MODEL), lambda i: (i, 0)),
            _resident((1, D_MODEL)),
            _resident((D_MODEL, D_FF)),
            _resident((D_FF, D_MODEL)),
            _resident((1, D_MODEL)),
        ],
        out_specs=pl.BlockSpec((tm, D_MODEL), lambda i: (i, 0)),
        out_shape=jax.ShapeDtypeStruct((m, D_MODEL), F32),
        compiler_params=pltpu.CompilerParams(
            dimension_semantics=("arbitrary",),
            vmem_limit_bytes=VMEM_LIMIT_BYTES),
        name="mlp",
    )(x2, norm_w, wup, wdown, norm_f)


def _pick(n, pref):
    t = min(pref, n)
    while n % t:
        t //= 2
    return t


def _reorder_w_in(w):
    o_dt = 3 * SC_WIDTH + SSM_D_INNER + SSM_CONV_DIM
    o_q = o_dt + SSM_HEADS
    o_gk = o_q + 2 * GLA_KEY_DIM + 2 * GLA_VAL_DIM
    o_gates = o_gk + GLA_GATE_RANK
    big = jnp.concatenate([w[:, :o_dt], w[:, o_q:o_gk], w[:, o_gates:]], axis=1)
    pad = jnp.zeros((w.shape[0], LANES - SSM_HEADS - GLA_GATE_RANK), w.dtype)
    small = jnp.concatenate([w[:, o_dt:o_q], w[:, o_gk:o_gates], pad], axis=1)
    return big.astype(BF16), small.astype(BF16)


def _pad_lanes(v, offset):
    out = jnp.zeros((1, LANES), F32)
    return out.at[0, offset:offset + v.shape[0]].set(v.astype(F32))


def kernel(x, norm_mix_w, w_in, conv_a_w, w_out_a, ssm_conv_w, ssm_conv_b, ssm_dt_bias, ssm_a_log, ssm_d,
           ssm_norm_w, w_out_ssm, gla_w_gk2, gla_b_gk, gla_norm_w, w_out_gla, w_o, norm_mlp_w, w_mlp_up,
           w_mlp_down, norm_f_w):
    b, s, d = x.shape
    m = b * s
    depth = w_in.shape[0]
    x2 = x.reshape(m, d).astype(F32)
    tm_proj = _pick(s, 256)
    ts_mix = _pick(s, 512)
    tm_mlp = _pick(m, 512)
    row = lambda v: v.reshape(1, -1).astype(F32)

    for i in range(depth):
        wbig, wsmall = _reorder_w_in(w_in[i])
        u, usmall = _proj(x2, row(norm_mix_w[i]), wbig, wsmall, conv_a_w[i].astype(F32),
                          ssm_conv_w[i].astype(F32), row(ssm_conv_b[i]), tm=tm_proj, seq=s)
        wgk_full = jnp.zeros((LANES, GLA_KEY_DIM), F32).at[
            SMALL_GK_LANE:SMALL_GK_LANE + GLA_GATE_RANK].set(gla_w_gk2[i].astype(F32))
        x2 = _mixer(
            x2, u, usmall,
            _pad_lanes(ssm_dt_bias[i], SMALL_DT_LANE), _pad_lanes(ssm_a_log[i], SMALL_DT_LANE),
            row(jnp.repeat(ssm_d[i], SSM_HEADDIM)), row(ssm_norm_w[i]),
            wgk_full, row(gla_b_gk[i]), row(gla_norm_w[i]),
            w_out_a[i].astype(BF16), w_out_ssm[i].astype(BF16), w_out_gla[i].astype(BF16),
            w_o[i].astype(BF16), ts=ts_mix, batch=b, seq=s)
        x2 = _mlp(x2, row(norm_mlp_w[i]), w_mlp_up[i].astype(BF16), w_mlp_down[i].astype(BF16),
                  row(norm_f_w), tm=tm_mlp, final_norm=(i == depth - 1))
    return x2.reshape(b, s, d)
```

```python
import functools

import jax
import jax.numpy as jnp
from jax import lax
from jax.experimental import pallas as pl
from jax.experimental.pallas import tpu as pltpu

F32 = jnp.float32
BF16 = jnp.bfloat16

EPS = 1e-6
D_MODEL = 1024
SC_WIDTH = 1024
SC_CONV_WIDTH = 3
SSM_D_INNER = 1024
SSM_HEADDIM = 64
SSM_HEADS = SSM_D_INNER // SSM_HEADDIM
SSM_GROUPS = 4
SSM_HPG = SSM_HEADS // SSM_GROUPS
SSM_STATE = 128
SSM_CONV_WIDTH = 4
SSM_CHUNK = 128
SSM_CONV_DIM = SSM_D_INNER + 2 * SSM_GROUPS * SSM_STATE
SSM_GROUP_WIDTH = SSM_D_INNER // SSM_GROUPS
GLA_HEADS = 4
GLA_KEY_DIM = D_MODEL // 2
GLA_VAL_DIM = D_MODEL
GLA_HEAD_K = GLA_KEY_DIM // GLA_HEADS
GLA_HEAD_V = GLA_VAL_DIM // GLA_HEADS
GLA_GATE_RANK = 16
GLA_GATE_NORMALIZER = 16.0
GLA_CHUNK = 64
N_BRANCHES = 3
D_FF = 4 * D_MODEL

LANES = 128
SUBLANES = 8
VMEM_LIMIT_BYTES = 56 * 1024 * 1024

PROJ_BLOCK = 1024
PB_AX, PB_AB, PB_AC, PB_Z, PB_XBC, PB_QK, PB_V, PB_G, PB_GATES = 0, 1, 2, 3, 4, 6, 7, 8, 9
N_PROJ_BLOCKS = 12
U_VA, U_Z, U_XBC, U_Q, U_K, U_V, U_G, U_GATES = 0, 1024, 2048, 4096, 4608, 5120, 6144, 7168
U_WIDTH = 10240
SMALL_DT_LANE = 0
SMALL_GK_LANE = GLA_GATE_RANK


def _split2(x):
    hi = x.astype(BF16)
    lo = (x - hi.astype(F32)).astype(BF16)
    return hi, lo


def _split3(x):
    hi = x.astype(BF16)
    r = x - hi.astype(F32)
    mid = r.astype(BF16)
    lo = (r - mid.astype(F32)).astype(BF16)
    return hi, mid, lo


def _dot(a, b):
    return jnp.dot(a, b, preferred_element_type=F32)


def _dot_nt(a, b):
    return lax.dot_general(a, b, (((1,), (1,)), ((), ())), preferred_element_type=F32)


def _dot_tn(a, b):
    return lax.dot_general(a, b, (((0,), (0,)), ((), ())), preferred_element_type=F32)


def _dot_exact_rhs(x, m):
    hi, mid, lo = _split3(x)
    return _dot(hi, m) + _dot(mid, m) + _dot(lo, m)


def _dot_exact_lhs(m, x):
    hi, mid, lo = _split3(x)
    return _dot(m, hi) + _dot(m, mid) + _dot(m, lo)


def _silu(x):
    return x * jax.nn.sigmoid(x)


def _softplus(x):
    return jnp.maximum(x, 0.0) + jnp.log1p(jnp.exp(-jnp.abs(x)))


def _log_sigmoid(x):
    return jnp.minimum(x, 0.0) - jnp.log1p(jnp.exp(-jnp.abs(x)))


def _rms(x):
    return x * lax.rsqrt(jnp.mean(x * x, axis=-1, keepdims=True) + EPS)


def _proj_kernel(x_ref, nw_ref, w_ref, wsmall_ref, cwa_ref, cwb_ref, cbb_ref, u_ref, usmall_ref,
                 work_ref, h_ref, *, tiles_per_seq):
    tm = x_ref.shape[0]
    first = (pl.program_id(0) % tiles_per_seq) == 0
    n_conv, n_slabs = work_ref.shape[0], work_ref.shape[1]

    @pl.when(first)
    def _():
        work_ref[:, :, 0:SUBLANES, :] = jnp.zeros((n_conv, n_slabs, SUBLANES, LANES), F32)

    @pl.when(jnp.logical_not(first))
    def _():
        work_ref[:, :, 0:SUBLANES, :] = work_ref[:, :, tm:tm + SUBLANES, :]

    h_ref[...] = (_rms(x_ref[...]) * nw_ref[...]).astype(BF16)
    usmall_ref[...] = _dot(h_ref[...], wsmall_ref[...])

    def proj(blk):
        return _dot(h_ref[...], w_ref[:, blk * PROJ_BLOCK:(blk + 1) * PROJ_BLOCK])

    def causal_conv(slot, pre, cw, width):
        for sl in range(n_slabs):
            work_ref[slot, sl, SUBLANES:SUBLANES + tm, :] = pre[:, sl * LANES:(sl + 1) * LANES]
        y = None
        for k in range(width):
            start = SUBLANES - (width - 1) + k
            shifted = jnp.concatenate(
                [work_ref[slot, sl, pl.ds(start, tm, stride=1), :] for sl in range(n_slabs)], axis=1)
            term = shifted * cw[k:k + 1]
            y = term if y is None else y + term
        return y

    def put(col, val):
        u_ref[:, col:col + PROJ_BLOCK] = val.astype(BF16)

    ax = proj(PB_AX)
    ac = proj(PB_AC)
    conv_a = causal_conv(0, ac * ax, cwa_ref[...], SC_CONV_WIDTH)
    put(U_VA, proj(PB_AB) * conv_a)
    put(U_Z, _silu(proj(PB_Z)))
    cwb = cwb_ref[...]
    cbb = cbb_ref[...]
    for c in range(SSM_CONV_DIM // PROJ_BLOCK):
        cols = slice(c * PROJ_BLOCK, (c + 1) * PROJ_BLOCK)
        y = causal_conv(1 + c, proj(PB_XBC + c), cwb[:, cols], SSM_CONV_WIDTH)
        put(U_XBC + c * PROJ_BLOCK, _silu(y + cbb[:, cols]))
    put(U_Q, proj(PB_QK))
    put(U_V, proj(PB_V))
    put(U_G, _silu(proj(PB_G)))
    for c in range(N_BRANCHES):
        put(U_GATES + c * PROJ_BLOCK, jax.nn.sigmoid(proj(PB_GATES + c)))


def _resident(shape):
    return pl.BlockSpec(shape, lambda *_: (0,) * len(shape), pipeline_mode=pl.Buffered(1))


def _proj(x2, norm_w, wbig, wsmall, conv_a_w, conv_b_w, conv_b_b, *, tm, seq):
    m = x2.shape[0]
    n_conv = 1 + SSM_CONV_DIM // PROJ_BLOCK
    return pl.pallas_call(
        functools.partial(_proj_kernel, tiles_per_seq=seq // tm),
        grid=(m // tm,),
        in_specs=[
            pl.BlockSpec((tm, D_MODEL), lambda i: (i, 0)),
            _resident((1, D_MODEL)),
            _resident((D_MODEL, N_PROJ_BLOCKS * PROJ_BLOCK)),
            _resident((D_MODEL, LANES)),
            _resident((SC_CONV_WIDTH, SC_WIDTH)),
            _resident((SSM_CONV_WIDTH, SSM_CONV_DIM)),
            _resident((1, SSM_CONV_DIM)),
        ],
        out_specs=[
            pl.BlockSpec((tm, U_WIDTH), lambda i: (i, 0)),
            pl.BlockSpec((tm, LANES), lambda i: (i, 0)),
        ],
        out_shape=[
            jax.ShapeDtypeStruct((m, U_WIDTH), BF16),
            jax.ShapeDtypeStruct((m, LANES), F32),
        ],
        scratch_shapes=[pltpu.VMEM((n_conv, PROJ_BLOCK // LANES, tm + SUBLANES, LANES), F32),
                        pltpu.VMEM((tm, D_MODEL), BF16)],
        compiler_params=pltpu.CompilerParams(
            dimension_semantics=("arbitrary",),
            vmem_limit_bytes=VMEM_LIMIT_BYTES),
        name="proj",
    )(x2, norm_w, wbig, wsmall, conv_a_w, conv_b_w, conv_b_b)


def _interleave(streams):
    streams = list(streams)
    while streams:
        for st in list(streams):
            try:
                next(st)
            except StopIteration:
                streams.remove(st)


def _ssd_consts(alog_ref):
    q = SSM_CHUNK
    row = lax.broadcasted_iota(jnp.int32, (q, q), 0)
    colm = lax.broadcasted_iota(jnp.int32, (q, q), 1)
    causal = row >= colm
    ltri = jnp.where(causal, 1.0, 0.0).astype(BF16)
    utri = jnp.where(row <= colm, 1.0, 0.0).astype(BF16)
    hrow = lax.broadcasted_iota(jnp.int32, (2 * LANES, SSM_D_INNER), 0) % LANES
    hcol = lax.broadcasted_iota(jnp.int32, (2 * LANES, SSM_D_INNER), 1)
    expand2 = jnp.where(hcol // SSM_HEADDIM == hrow, 1.0, 0.0).astype(BF16)
    lane = lax.broadcasted_iota(jnp.int32, (1, LANES), 1)
    a_full = jnp.where(lane < SSM_HEADS, -jnp.exp(alog_ref[...]), 0.0)
    glane = lax.broadcasted_iota(jnp.int32, (1, SSM_GROUP_WIDTH), 1)
    return causal, ltri, utri, expand2, a_full, glane


def _expand_heads(x, expand2):
    hi, lo = _split2(x)
    return _dot(jnp.concatenate([hi, lo], axis=1), expand2)


def _ssd_chunk(r0, consts, xbc_ref, zs_ref, small_ref, dtb, drow, nw, state_ref, y_ref):
    q = SSM_CHUNK
    gw = SSM_GROUP_WIDTH
    ns = SSM_STATE
    causal, ltri, utri, expand2, a_full, glane = consts

    xs = xbc_ref[pl.ds(r0, q), 0:SSM_D_INNER].astype(F32)
    dt = _softplus(small_ref[pl.ds(r0, q), :] + dtb)
    dta = dt * a_full
    acol = _dot_exact_lhs(ltri, dta)
    arow = _dot_exact_rhs(dta.T, utri)
    dt_x = _expand_heads(dt, expand2)
    yield
    alast = acol[q - 1:q, :]
    ea_x = _expand_heads(jnp.exp(acol), expand2)
    dte_x = _expand_heads(jnp.exp(alast - acol), expand2)
    xdt = xs * dt_x
    xdt_b = xdt.astype(BF16)
    yield
    cdec_x = ea_x[q - 1:q, :]
    xdte_b = (xdt * dte_x).astype(BF16)

    for g in range(SSM_GROUPS):
        gcols = slice(g * gw, (g + 1) * gw)
        bm_g = xbc_ref[pl.ds(r0, q), SSM_D_INNER + g * ns:SSM_D_INNER + (g + 1) * ns]
        cm_g = xbc_ref[pl.ds(r0, q), SSM_D_INNER + (SSM_GROUPS + g) * ns:
                       SSM_D_INNER + (SSM_GROUPS + g + 1) * ns]
        cbm = _dot_nt(cm_g, bm_g)
        st = state_ref[g]
        y_off = _dot(cm_g, st.astype(BF16)) * ea_x[:, gcols]
        state_ref[g] = st * cdec_x[:, gcols] + _dot_tn(bm_g, xdte_b[:, gcols])
        yield
        xg = xdt_b[:, gcols]
        lhs, rhs = [], []
        for r in range(SSM_HPG):
            h = g * SSM_HPG + r
            seg = acol[:, h:h + 1] - arow[h:h + 1, :]
            dec = jnp.where(causal, jnp.exp(jnp.where(causal, seg, 0.0)), 0.0)
            lhs.append((cbm * dec).astype(BF16))
            rhs.append(jnp.where(glane // SSM_HEADDIM == r, xg, jnp.zeros_like(xg)))
        y_diag = _dot(jnp.concatenate(lhs, axis=1), jnp.concatenate(rhs, axis=0))
        yield
        yg = y_diag + y_off + xs[:, gcols] * drow[:, gcols]
        yg = _rms(yg * zs_ref[pl.ds(r0, q), gcols].astype(F32))
        y_ref[pl.ds(r0, q), gcols] = (yg * nw[:, gcols]).astype(BF16)


def _gla_chunk(r0, causal, ltri, q_ref, k_ref, v_ref, gs_ref, gk_ref, nw, state_ref, o_ref):
    t = GLA_CHUNK
    dk, dv = GLA_HEAD_K, GLA_HEAD_V
    scale = dk ** -0.5
    gcum = _dot_exact_lhs(ltri, gk_ref[pl.ds(r0, t), :])
    yield
    glast = gcum[t - 1:t, :]
    qf = q_ref[pl.ds(r0, t), :].astype(F32)
    kf = k_ref[pl.ds(r0, t), :].astype(F32)
    q_in = (qf * scale * jnp.exp(gcum)).astype(BF16)
    k_in = (kf * jnp.exp(-gcum)).astype(BF16)
    k_end = (kf * jnp.exp(glast - gcum)).astype(BF16)
    cdec = jnp.exp(glast)
    for h in range(GLA_HEADS):
        kcols = slice(h * dk, (h + 1) * dk)
        vcols = slice(h * dv, (h + 1) * dv)
        qh = q_in[:, kcols]
        vh = v_ref[pl.ds(r0, t), vcols]
        sc = _dot_nt(qh, k_in[:, kcols])
        st = state_ref[h]
        o_inter = _dot_nt(qh, st.astype(BF16))
        state_ref[h] = st * cdec[:, kcols] + _dot_tn(vh, k_end[:, kcols])
        yield
        o = _dot(jnp.where(causal, sc, 0.0).astype(BF16), vh) + o_inter
        yield
        o = _rms(o) * nw * gs_ref[pl.ds(r0, t), vcols].astype(F32)
        o_ref[pl.ds(r0, t), vcols] = o.astype(BF16)


def _mixer_kernel(x_ref, va_ref, zs_ref, xbc_ref, q_ref, k_ref, v_ref, gs_ref, ga_ref, gb_ref, gc_ref,
                  small_ref, dtb_ref, alog_ref, drow_ref, snw_ref, wgk_ref, bgk_ref, gnw_ref,
                  wa_ref, wb_ref, wc_ref, wo_ref, out_ref, sstate_ref, gstate_ref, y_ref, o_ref, gk_ref):
    ts = x_ref.shape[0]

    @pl.when(pl.program_id(1) == 0)
    def _():
        sstate_ref[...] = jnp.zeros_like(sstate_ref)
        gstate_ref[...] = jnp.zeros_like(gstate_ref)

    w_hi, w_lo = _split2(wgk_ref[...])
    s_hi, s_lo = _split2(small_ref[...])
    zg = _dot(s_hi, w_hi) + _dot(s_hi, w_lo) + _dot(s_lo, w_hi) + bgk_ref[...]
    gk_ref[...] = _log_sigmoid(zg) / GLA_GATE_NORMALIZER

    consts = _ssd_consts(alog_ref)
    t = GLA_CHUNK
    grow = lax.broadcasted_iota(jnp.int32, (t, t), 0)
    gcol = lax.broadcasted_iota(jnp.int32, (t, t), 1)
    gcausal = grow >= gcol
    gltri = jnp.where(gcausal, 1.0, 0.0).astype(BF16)
    dtb = dtb_ref[...]
    drow = drow_ref[...]
    snw = snw_ref[...]
    gnw = gnw_ref[...]
    gla_per_ssd = SSM_CHUNK // GLA_CHUNK

    def step(c, carry):
        r0 = pl.multiple_of(c * SSM_CHUNK, SSM_CHUNK)

        def gla_stream():
            for sub in range(gla_per_ssd):
                yield from _gla_chunk(pl.multiple_of(r0 + sub * t, t), gcausal, gltri, q_ref, k_ref, v_ref,
                                      gs_ref, gk_ref, gnw, gstate_ref, o_ref)

        _interleave([_ssd_chunk(r0, consts, xbc_ref, zs_ref, small_ref, dtb, drow, snw, sstate_ref, y_ref),
                     gla_stream()])
        return carry

    lax.fori_loop(0, ts // SSM_CHUNK, step, 0)

    merged = (ga_ref[...].astype(F32) * _dot(va_ref[...], wa_ref[...])
              + gb_ref[...].astype(F32) * _dot(y_ref[...], wb_ref[...])
              + gc_ref[...].astype(F32) * _dot(o_ref[...], wc_ref[...]))
    out_ref[...] = x_ref[...] + _dot(merged.astype(BF16), wo_ref[...])


def _mixer(x2, u, usmall, dt_bias, a_log, d_row, ssm_nw, wgk_full, b_gk, gla_nw, wa, wb, wc, wo,
           *, ts, batch, seq):
    m = x2.shape[0]
    tps = seq // ts

    def rows(w, c):
        return pl.BlockSpec((ts, w), lambda b, j: (b * tps + j, c // w))

    return pl.pallas_call(
        _mixer_kernel,
        grid=(batch, tps),
        in_specs=[
            rows(D_MODEL, 0),
            rows(SC_WIDTH, U_VA),
            rows(SSM_D_INNER, U_Z),
            rows(SSM_CONV_DIM, U_XBC),
            rows(GLA_KEY_DIM, U_Q),
            rows(GLA_KEY_DIM, U_K),
            rows(GLA_VAL_DIM, U_V),
            rows(GLA_VAL_DIM, U_G),
            rows(D_MODEL, U_GATES),
            rows(D_MODEL, U_GATES + D_MODEL),
            rows(D_MODEL, U_GATES + 2 * D_MODEL),
            rows(LANES, 0),
            _resident((1, LANES)),
            _resident((1, LANES)),
            _resident((1, SSM_D_INNER)),
            _resident((1, SSM_D_INNER)),
            _resident((LANES, GLA_KEY_DIM)),
            _resident((1, GLA_KEY_DIM)),
            _resident((1, GLA_HEAD_V)),
            _resident((SC_WIDTH, D_MODEL)),
            _resident((SSM_D_INNER, D_MODEL)),
            _resident((GLA_VAL_DIM, D_MODEL)),
            _resident((D_MODEL, D_MODEL)),
        ],
        out_specs=pl.BlockSpec((ts, D_MODEL), lambda b, j: (b * tps + j, 0)),
        out_shape=jax.ShapeDtypeStruct((m, D_MODEL), F32),
        scratch_shapes=[
            pltpu.VMEM((SSM_GROUPS, SSM_STATE, SSM_GROUP_WIDTH), F32),
            pltpu.VMEM((GLA_HEADS, GLA_HEAD_V, GLA_HEAD_K), F32),
            pltpu.VMEM((ts, SSM_D_INNER), BF16),
            pltpu.VMEM((ts, GLA_VAL_DIM), BF16),
            pltpu.VMEM((ts, GLA_KEY_DIM), F32),
        ],
        compiler_params=pltpu.CompilerParams(
            dimension_semantics=("arbitrary", "arbitrary"),
            vmem_limit_bytes=VMEM_LIMIT_BYTES),
        name="mixer",
    )(x2, u, u, u, u, u, u, u, u, u, u, usmall, dt_bias, a_log, d_row, ssm_nw, wgk_full, b_gk, gla_nw,
      wa, wb, wc, wo)


def _mlp_kernel(x_ref, nw_ref, wup_ref, wdown_ref, nf_ref, out_ref, *, ff_chunk, final_norm):
    x = x_ref[...]
    hb = (_rms(x) * nw_ref[...]).astype(BF16)
    acc = x
    for c in range(D_FF // ff_chunk):
        up = _dot(hb, wup_ref[:, c * ff_chunk:(c + 1) * ff_chunk])
        act = jnp.square(jnp.maximum(up, 0.0)).astype(BF16)
        acc = acc + _dot(act, wdown_ref[c * ff_chunk:(c + 1) * ff_chunk, :])
    if final_norm:
        acc = _rms(acc) * nf_ref[...]
    out_ref[...] = acc


def _mlp(x2, norm_w, wup, wdown, norm_f, *, tm, final_norm):
    m = x2.shape[0]
    return pl.pallas_call(
        functools.partial(_mlp_kernel, ff_chunk=1024, final_norm=final_norm),
        grid=(m // tm,),
        in_specs=[
            pl.BlockSpec((tm, D_MODEL), lambda i: (i, 0)),
            _resident((1, D_MODEL)),
            _resident((D_MODEL, D_FF)),
            _resident((D_FF, D_MODEL)),
            _resident((1, D_MODEL)),
        ],
        out_specs=pl.BlockSpec((tm, D_MODEL), lambda i: (i, 0)),
        out_shape=jax.ShapeDtypeStruct((m, D_MODEL), F32),
        compiler_params=pltpu.CompilerParams(
            dimension_semantics=("arbitrary",),
            vmem_limit_bytes=VMEM_LIMIT_BYTES),
        name="mlp",
    )(x2, norm_w, wup, wdown, norm_f)


def _pick(n, pref):
    t = min(pref, n)
    while n % t:
        t //= 2
    return t


def _reorder_w_in(w):
    o_dt = 3 * SC_WIDTH + SSM_D_INNER + SSM_CONV_DIM
    o_q = o_dt + SSM_HEADS
    o_gk = o_q + 2 * GLA_KEY_DIM + 2 * GLA_VAL_DIM
    o_gates = o_gk + GLA_GATE_RANK
    big = jnp.concatenate([w[..., :o_dt], w[..., o_q:o_gk], w[..., o_gates:]], axis=-1)
    pad = jnp.zeros(w.shape[:-1] + (LANES - SSM_HEADS - GLA_GATE_RANK,), w.dtype)
    small = jnp.concatenate([w[..., o_dt:o_q], w[..., o_gk:o_gates], pad], axis=-1)
    return big.astype(BF16), small.astype(BF16)


def _pad_lanes(v, offset):
    v = v.astype(F32)
    return jnp.pad(v, ((0, 0), (offset, LANES - offset - v.shape[-1])))[:, None, :]


def kernel(x, norm_mix_w, w_in, conv_a_w, w_out_a, ssm_conv_w, ssm_conv_b, ssm_dt_bias, ssm_a_log, ssm_d,
           ssm_norm_w, w_out_ssm, gla_w_gk2, gla_b_gk, gla_norm_w, w_out_gla, w_o, norm_mlp_w, w_mlp_up,
           w_mlp_down, norm_f_w):
    b, s, d = x.shape
    m = b * s
    depth = w_in.shape[0]
    x2 = x.reshape(m, d).astype(F32)
    tm_proj = _pick(s, 256)
    ts_mix = _pick(s, 512)
    tm_mlp = _pick(m, 512)
    rows = lambda v: v.astype(F32)[:, None, :]

    wbig, wsmall = _reorder_w_in(w_in)
    wa, wb, wc, wo = (w.astype(BF16) for w in (w_out_a, w_out_ssm, w_out_gla, w_o))
    wup, wdown = w_mlp_up.astype(BF16), w_mlp_down.astype(BF16)
    nmix, nmlp = rows(norm_mix_w), rows(norm_mlp_w)
    cwa, cwb, cbb = conv_a_w.astype(F32), ssm_conv_w.astype(F32), rows(ssm_conv_b)
    dtb, alog = _pad_lanes(ssm_dt_bias, SMALL_DT_LANE), _pad_lanes(ssm_a_log, SMALL_DT_LANE)
    drow, snw = rows(jnp.repeat(ssm_d, SSM_HEADDIM, axis=-1)), rows(ssm_norm_w)
    wgk = jnp.pad(gla_w_gk2.astype(F32),
                  ((0, 0), (SMALL_GK_LANE, LANES - SMALL_GK_LANE - GLA_GATE_RANK), (0, 0)))
    bgk, gnw = rows(gla_b_gk), rows(gla_norm_w)
    nf = norm_f_w.reshape(1, -1).astype(F32)

    for i in range(depth):
        u, usmall = _proj(x2, nmix[i], wbig[i], wsmall[i], cwa[i], cwb[i], cbb[i], tm=tm_proj, seq=s)
        x2 = _mixer(x2, u, usmall, dtb[i], alog[i], drow[i], snw[i], wgk[i], bgk[i], gnw[i],
                    wa[i], wb[i], wc[i], wo[i], ts=ts_mix, batch=b, seq=s)
        x2 = _mlp(x2, nmlp[i], wup[i], wdown[i], nf, tm=tm_mlp, final_norm=(i == depth - 1))
    return x2.reshape(b, s, d)
```

```python
import functools

import jax
import jax.numpy as jnp
from jax import lax
from jax.experimental import pallas as pl
from jax.experimental.pallas import tpu as pltpu

F32 = jnp.float32
BF16 = jnp.bfloat16

EPS = 1e-6
D_MODEL = 1024
SC_WIDTH = 1024
SC_CONV_WIDTH = 3
SSM_D_INNER = 1024
SSM_HEADDIM = 64
SSM_HEADS = SSM_D_INNER // SSM_HEADDIM
SSM_GROUPS = 4
SSM_HPG = SSM_HEADS // SSM_GROUPS
SSM_STATE = 128
SSM_CONV_WIDTH = 4
SSM_CHUNK = 128
SSM_CONV_DIM = SSM_D_INNER + 2 * SSM_GROUPS * SSM_STATE
SSM_GROUP_WIDTH = SSM_D_INNER // SSM_GROUPS
GLA_HEADS = 4
GLA_KEY_DIM = D_MODEL // 2
GLA_VAL_DIM = D_MODEL
GLA_HEAD_K = GLA_KEY_DIM // GLA_HEADS
GLA_HEAD_V = GLA_VAL_DIM // GLA_HEADS
GLA_GATE_RANK = 16
GLA_GATE_NORMALIZER = 16.0
GLA_CHUNK = 64
N_BRANCHES = 3
D_FF = 4 * D_MODEL

LANES = 128
SUBLANES = 8
VMEM_LIMIT_BYTES = 56 * 1024 * 1024

PROJ_BLOCK = 1024
PB_AX, PB_AB, PB_AC, PB_Z, PB_XBC, PB_QK, PB_V, PB_G, PB_GATES = 0, 1, 2, 3, 4, 6, 7, 8, 9
N_PROJ_BLOCKS = 12
U_VA, U_Z, U_XBC, U_Q, U_K, U_V, U_G, U_GATES = 0, 1024, 2048, 4096, 4608, 5120, 6144, 7168
U_WIDTH = 10240
SMALL_DT_LANE = 0
SMALL_GK_LANE = GLA_GATE_RANK


def _split2(x):
    hi = x.astype(BF16)
    lo = (x - hi.astype(F32)).astype(BF16)
    return hi, lo


def _split3(x):
    hi = x.astype(BF16)
    r = x - hi.astype(F32)
    mid = r.astype(BF16)
    lo = (r - mid.astype(F32)).astype(BF16)
    return hi, mid, lo


def _dot(a, b):
    return jnp.dot(a, b, preferred_element_type=F32)


def _dot_nt(a, b):
    return lax.dot_general(a, b, (((1,), (1,)), ((), ())), preferred_element_type=F32)


def _dot_tn(a, b):
    return lax.dot_general(a, b, (((0,), (0,)), ((), ())), preferred_element_type=F32)


def _dot_exact_rhs(x, m):
    hi, mid, lo = _split3(x)
    return _dot(hi, m) + _dot(mid, m) + _dot(lo, m)


def _dot_exact_lhs(m, x):
    hi, mid, lo = _split3(x)
    return _dot(m, hi) + _dot(m, mid) + _dot(m, lo)


def _silu(x):
    return x * jax.nn.sigmoid(x)


def _softplus(x):
    return jnp.maximum(x, 0.0) + jnp.log1p(jnp.exp(-jnp.abs(x)))


def _log_sigmoid(x):
    return jnp.minimum(x, 0.0) - jnp.log1p(jnp.exp(-jnp.abs(x)))


def _rms(x):
    return x * lax.rsqrt(jnp.mean(x * x, axis=-1, keepdims=True) + EPS)


def _proj_kernel(x_ref, nw_ref, w_ref, wsmall_ref, cwa_ref, cwb_ref, cbb_ref, u_ref, usmall_ref,
                 work_ref, h_ref, *, tiles_per_seq):
    tm = x_ref.shape[0]
    first = (pl.program_id(0) % tiles_per_seq) == 0
    n_conv, n_slabs = work_ref.shape[0], work_ref.shape[1]

    @pl.when(first)
    def _():
        work_ref[:, :, 0:SUBLANES, :] = jnp.zeros((n_conv, n_slabs, SUBLANES, LANES), F32)

    @pl.when(jnp.logical_not(first))
    def _():
        work_ref[:, :, 0:SUBLANES, :] = work_ref[:, :, tm:tm + SUBLANES, :]

    h_ref[...] = (_rms(x_ref[...]) * nw_ref[...]).astype(BF16)
    usmall_ref[...] = _dot(h_ref[...], wsmall_ref[...])

    def proj(blk):
        return _dot(h_ref[...], w_ref[:, blk * PROJ_BLOCK:(blk + 1) * PROJ_BLOCK])

    def causal_conv(slot, pre, cw, width):
        for sl in range(n_slabs):
            work_ref[slot, sl, SUBLANES:SUBLANES + tm, :] = pre[:, sl * LANES:(sl + 1) * LANES]
        y = None
        for k in range(width):
            start = SUBLANES - (width - 1) + k
            shifted = jnp.concatenate(
                [work_ref[slot, sl, pl.ds(start, tm, stride=1), :] for sl in range(n_slabs)], axis=1)
            term = shifted * cw[k:k + 1]
            y = term if y is None else y + term
        return y

    def put(col, val):
        u_ref[:, col:col + PROJ_BLOCK] = val.astype(BF16)

    ax = proj(PB_AX)
    ac = proj(PB_AC)
    conv_a = causal_conv(0, ac * ax, cwa_ref[...], SC_CONV_WIDTH)
    put(U_VA, proj(PB_AB) * conv_a)
    put(U_Z, _silu(proj(PB_Z)))
    cwb = cwb_ref[...]
    cbb = cbb_ref[...]
    for c in range(SSM_CONV_DIM // PROJ_BLOCK):
        cols = slice(c * PROJ_BLOCK, (c + 1) * PROJ_BLOCK)
        y = causal_conv(1 + c, proj(PB_XBC + c), cwb[:, cols], SSM_CONV_WIDTH)
        put(U_XBC + c * PROJ_BLOCK, _silu(y + cbb[:, cols]))
    put(U_Q, proj(PB_QK))
    put(U_V, proj(PB_V))
    put(U_G, _silu(proj(PB_G)))
    for c in range(N_BRANCHES):
        put(U_GATES + c * PROJ_BLOCK, jax.nn.sigmoid(proj(PB_GATES + c)))


def _resident(shape):
    return pl.BlockSpec(shape, lambda *_: (0,) * len(shape), pipeline_mode=pl.Buffered(1))


def _proj(x2, norm_w, wbig, wsmall, conv_a_w, conv_b_w, conv_b_b, *, tm, seq):
    m = x2.shape[0]
    n_conv = 1 + SSM_CONV_DIM // PROJ_BLOCK
    return pl.pallas_call(
        functools.partial(_proj_kernel, tiles_per_seq=seq // tm),
        grid=(m // tm,),
        in_specs=[
            pl.BlockSpec((tm, D_MODEL), lambda i: (i, 0)),
            _resident((1, D_MODEL)),
            _resident((D_MODEL, N_PROJ_BLOCKS * PROJ_BLOCK)),
            _resident((D_MODEL, LANES)),
            _resident((SC_CONV_WIDTH, SC_WIDTH)),
            _resident((SSM_CONV_WIDTH, SSM_CONV_DIM)),
            _resident((1, SSM_CONV_DIM)),
        ],
        out_specs=[
            pl.BlockSpec((tm, U_WIDTH), lambda i: (i, 0)),
            pl.BlockSpec((tm, LANES), lambda i: (i, 0)),
        ],
        out_shape=[
            jax.ShapeDtypeStruct((m, U_WIDTH), BF16),
            jax.ShapeDtypeStruct((m, LANES), F32),
        ],
        scratch_shapes=[pltpu.VMEM((n_conv, PROJ_BLOCK // LANES, tm + SUBLANES, LANES), F32),
                        pltpu.VMEM((tm, D_MODEL), BF16)],
        compiler_params=pltpu.CompilerParams(
            dimension_semantics=("arbitrary",),
            vmem_limit_bytes=VMEM_LIMIT_BYTES),
        name="proj",
    )(x2, norm_w, wbig, wsmall, conv_a_w, conv_b_w, conv_b_b)


def _interleave(streams):
    streams = list(streams)
    while streams:
        for st in list(streams):
            try:
                next(st)
            except StopIteration:
                streams.remove(st)


def _ssd_consts(alog_ref):
    q = SSM_CHUNK
    row = lax.broadcasted_iota(jnp.int32, (q, q), 0)
    colm = lax.broadcasted_iota(jnp.int32, (q, q), 1)
    causal = row >= colm
    ltri = jnp.where(causal, 1.0, 0.0).astype(BF16)
    utri = jnp.where(row <= colm, 1.0, 0.0).astype(BF16)
    hrow = lax.broadcasted_iota(jnp.int32, (2 * LANES, SSM_D_INNER), 0) % LANES
    hcol = lax.broadcasted_iota(jnp.int32, (2 * LANES, SSM_D_INNER), 1)
    expand2 = jnp.where(hcol // SSM_HEADDIM == hrow, 1.0, 0.0).astype(BF16)
    lane = lax.broadcasted_iota(jnp.int32, (1, LANES), 1)
    a_full = jnp.where(lane < SSM_HEADS, -jnp.exp(alog_ref[...]), 0.0)
    glane = lax.broadcasted_iota(jnp.int32, (1, SSM_GROUP_WIDTH), 1)
    return causal, ltri, utri, expand2, a_full, glane


def _ssd_chunk(r0, consts, xbc_ref, zs_ref, small_ref, dtb, drow, nw, state_ref, y_ref):
    q = SSM_CHUNK
    gw = SSM_GROUP_WIDTH
    ns = SSM_STATE
    causal, ltri, utri, expand2, a_full, glane = consts

    dt = _softplus(small_ref[pl.ds(r0, q), :] + dtb)
    dta = dt * a_full
    acol = _dot_exact_lhs(ltri, dta)
    arow = _dot_exact_rhs(dta.T, utri)
    dt_row = dt.T
    yield
    alast = acol[q - 1:q, :]
    ea_cat = jnp.concatenate(_split2(jnp.exp(acol)), axis=1)
    w_cat = jnp.concatenate(_split2(dt * jnp.exp(alast - acol)), axis=1)
    yield

    def expand_group(g):
        gcols = slice(g * gw, (g + 1) * gw)
        return _dot(ea_cat, expand2[:, gcols]), _dot(w_cat, expand2[:, gcols])

    nxt = expand_group(0)
    for g in range(SSM_GROUPS):
        gcols = slice(g * gw, (g + 1) * gw)
        ea_g, w_g = nxt
        if g + 1 < SSM_GROUPS:
            nxt = expand_group(g + 1)
        xs_g = xbc_ref[pl.ds(r0, q), gcols]
        xs_f = xs_g.astype(F32)
        bm_g = xbc_ref[pl.ds(r0, q), SSM_D_INNER + g * ns:SSM_D_INNER + (g + 1) * ns]
        cm_g = xbc_ref[pl.ds(r0, q), SSM_D_INNER + (SSM_GROUPS + g) * ns:
                       SSM_D_INNER + (SSM_GROUPS + g + 1) * ns]
        cbm = _dot_nt(cm_g, bm_g)
        st = state_ref[g]
        y_off = _dot(cm_g, st.astype(BF16)) * ea_g
        state_ref[g] = st * ea_g[q - 1:q, :] + _dot_tn(bm_g, (xs_f * w_g).astype(BF16))
        yield
        lhs, rhs = [], []
        for r in range(SSM_HPG):
            h = g * SSM_HPG + r
            seg = acol[:, h:h + 1] - arow[h:h + 1, :]
            dec = jnp.where(causal, jnp.exp(seg), 0.0) * dt_row[h:h + 1, :]
            lhs.append((cbm * dec).astype(BF16))
            rhs.append(jnp.where(glane // SSM_HEADDIM == r, xs_g, jnp.zeros_like(xs_g)))
        y_diag = _dot(jnp.concatenate(lhs, axis=1), jnp.concatenate(rhs, axis=0))
        yield
        yg = y_diag + y_off + xs_f * drow[:, gcols]
        yg = _rms(yg * zs_ref[pl.ds(r0, q), gcols].astype(F32))
        y_ref[pl.ds(r0, q), gcols] = (yg * nw[:, gcols]).astype(BF16)


def _gla_chunk(r0, causal, ltri, q_ref, k_ref, v_ref, gs_ref, gk_ref, nw, state_ref, o_ref):
    t = GLA_CHUNK
    dk, dv = GLA_HEAD_K, GLA_HEAD_V
    scale = dk ** -0.5
    gcum = _dot_exact_lhs(ltri, gk_ref[pl.ds(r0, t), :])
    yield
    glast = gcum[t - 1:t, :]
    qf = q_ref[pl.ds(r0, t), :].astype(F32)
    kf = k_ref[pl.ds(r0, t), :].astype(F32)
    q_in = (qf * scale * jnp.exp(gcum)).astype(BF16)
    k_in = (kf * jnp.exp(-gcum)).astype(BF16)
    k_end = (kf * jnp.exp(glast - gcum)).astype(BF16)
    cdec = jnp.exp(glast)
    for h in range(GLA_HEADS):
        kcols = slice(h * dk, (h + 1) * dk)
        vcols = slice(h * dv, (h + 1) * dv)
        qh = q_in[:, kcols]
        vh = v_ref[pl.ds(r0, t), vcols]
        sc = _dot_nt(qh, k_in[:, kcols])
        st = state_ref[h]
        o_inter = _dot_nt(qh, st.astype(BF16))
        state_ref[h] = st * cdec[:, kcols] + _dot_tn(vh, k_end[:, kcols])
        yield
        o = _dot(jnp.where(causal, sc, 0.0).astype(BF16), vh) + o_inter
        yield
        o = _rms(o) * nw * gs_ref[pl.ds(r0, t), vcols].astype(F32)
        o_ref[pl.ds(r0, t), vcols] = o.astype(BF16)


def _mixer_kernel(x_ref, va_ref, zs_ref, xbc_ref, q_ref, k_ref, v_ref, gs_ref, ga_ref, gb_ref, gc_ref,
                  small_ref, dtb_ref, alog_ref, drow_ref, snw_ref, wgk_ref, bgk_ref, gnw_ref,
                  wa_ref, wb_ref, wc_ref, wo_ref, out_ref, sstate_ref, gstate_ref, y_ref, o_ref, gk_ref):
    ts = x_ref.shape[0]

    @pl.when(pl.program_id(1) == 0)
    def _():
        sstate_ref[...] = jnp.zeros_like(sstate_ref)
        gstate_ref[...] = jnp.zeros_like(gstate_ref)

    w_hi, w_lo = _split2(wgk_ref[...])
    s_hi, s_lo = _split2(small_ref[...])
    zg = _dot(s_hi, w_hi) + _dot(s_hi, w_lo) + _dot(s_lo, w_hi) + bgk_ref[...]
    gk_ref[...] = _log_sigmoid(zg) / GLA_GATE_NORMALIZER

    consts = _ssd_consts(alog_ref)
    t = GLA_CHUNK
    grow = lax.broadcasted_iota(jnp.int32, (t, t), 0)
    gcol = lax.broadcasted_iota(jnp.int32, (t, t), 1)
    gcausal = grow >= gcol
    gltri = jnp.where(gcausal, 1.0, 0.0).astype(BF16)
    dtb = dtb_ref[...]
    drow = drow_ref[...]
    snw = snw_ref[...]
    gnw = gnw_ref[...]
    gla_per_ssd = SSM_CHUNK // GLA_CHUNK

    def step(c, carry):
        r0 = pl.multiple_of(c * SSM_CHUNK, SSM_CHUNK)

        def gla_stream():
            for sub in range(gla_per_ssd):
                yield from _gla_chunk(pl.multiple_of(r0 + sub * t, t), gcausal, gltri, q_ref, k_ref, v_ref,
                                      gs_ref, gk_ref, gnw, gstate_ref, o_ref)

        _interleave([_ssd_chunk(r0, consts, xbc_ref, zs_ref, small_ref, dtb, drow, snw, sstate_ref, y_ref),
                     gla_stream()])
        return carry

    lax.fori_loop(0, ts // SSM_CHUNK, step, 0)

    merged = (ga_ref[...].astype(F32) * _dot(va_ref[...], wa_ref[...])
              + gb_ref[...].astype(F32) * _dot(y_ref[...], wb_ref[...])
              + gc_ref[...].astype(F32) * _dot(o_ref[...], wc_ref[...]))
    out_ref[...] = x_ref[...] + _dot(merged.astype(BF16), wo_ref[...])


def _mixer(x2, u, usmall, dt_bias, a_log, d_row, ssm_nw, wgk_full, b_gk, gla_nw, wa, wb, wc, wo,
           *, ts, batch, seq):
    m = x2.shape[0]
    tps = seq // ts

    def rows(w, c):
        return pl.BlockSpec((ts, w), lambda b, j: (b * tps + j, c // w))

    return pl.pallas_call(
        _mixer_kernel,
        grid=(batch, tps),
        in_specs=[
            rows(D_MODEL, 0),
            rows(SC_WIDTH, U_VA),
            rows(SSM_D_INNER, U_Z),
            rows(SSM_CONV_DIM, U_XBC),
            rows(GLA_KEY_DIM, U_Q),
            rows(GLA_KEY_DIM, U_K),
            rows(GLA_VAL_DIM, U_V),
            rows(GLA_VAL_DIM, U_G),
            rows(D_MODEL, U_GATES),
            rows(D_MODEL, U_GATES + D_MODEL),
            rows(D_MODEL, U_GATES + 2 * D_MODEL),
            rows(LANES, 0),
            _resident((1, LANES)),
            _resident((1, LANES)),
            _resident((1, SSM_D_INNER)),
            _resident((1, SSM_D_INNER)),
            _resident((LANES, GLA_KEY_DIM)),
            _resident((1, GLA_KEY_DIM)),
            _resident((1, GLA_HEAD_V)),
            _resident((SC_WIDTH, D_MODEL)),
            _resident((SSM_D_INNER, D_MODEL)),
            _resident((GLA_VAL_DIM, D_MODEL)),
            _resident((D_MODEL, D_MODEL)),
        ],
        out_specs=pl.BlockSpec((ts, D_MODEL), lambda b, j: (b * tps + j, 0)),
        out_shape=jax.ShapeDtypeStruct((m, D_MODEL), F32),
        scratch_shapes=[
            pltpu.VMEM((SSM_GROUPS, SSM_STATE, SSM_GROUP_WIDTH), F32),
            pltpu.VMEM((GLA_HEADS, GLA_HEAD_V, GLA_HEAD_K), F32),
            pltpu.VMEM((ts, SSM_D_INNER), BF16),
            pltpu.VMEM((ts, GLA_VAL_DIM), BF16),
            pltpu.VMEM((ts, GLA_KEY_DIM), F32),
        ],
        compiler_params=pltpu.CompilerParams(
            dimension_semantics=("arbitrary", "arbitrary"),
            vmem_limit_bytes=VMEM_LIMIT_BYTES),
        name="mixer",
    )(x2, u, u, u, u, u, u, u, u, u, u, usmall, dt_bias, a_log, d_row, ssm_nw, wgk_full, b_gk, gla_nw,
      wa, wb, wc, wo)


def _mlp_kernel(x_ref, nw_ref, wup_ref, wdown_ref, nf_ref, out_ref, *, ff_chunk, final_norm):
    x = x_ref[...]
    hb = (_rms(x) * nw_ref[...]).astype(BF16)
    acc = x
    for c in range(D_FF // ff_chunk):
        up = _dot(hb, wup_ref[:, c * ff_chunk:(c + 1) * ff_chunk])
        act = jnp.square(jnp.maximum(up, 0.0)).astype(BF16)
        acc = acc + _dot(act, wdown_ref[c * ff_chunk:(c + 1) * ff_chunk, :])
    if final_norm:
        acc = _rms(acc) * nf_ref[...]
    out_ref[...] = acc


def _mlp(x2, norm_w, wup, wdown, norm_f, *, tm, final_norm):
    m = x2.shape[0]
    return pl.pallas_call(
        functools.partial(_mlp_kernel, ff_chunk=1024, final_norm=final_norm),
        grid=(m // tm,),
        in_specs=[
            pl.BlockSpec((tm, D_MODEL), lambda i: (i, 0)),
            _resident((1, D_MODEL)),
            _resident((D_MODEL, D_FF)),
            _resident((D_FF, D_MODEL)),
            _resident((1, D_MODEL)),
        ],
        out_specs=pl.BlockSpec((tm, D_MODEL), lambda i: (i, 0)),
        out_shape=jax.ShapeDtypeStruct((m, D_MODEL), F32),
        compiler_params=pltpu.CompilerParams(
            dimension_semantics=("arbitrary",),
            vmem_limit_bytes=VMEM_LIMIT_BYTES),
        name="mlp",
    )(x2, norm_w, wup, wdown, norm_f)


def _pick(n, pref):
    t = min(pref, n)
    while n % t:
        t //= 2
    return t


def _reorder_w_in(w):
    o_dt = 3 * SC_WIDTH + SSM_D_INNER + SSM_CONV_DIM
    o_q = o_dt + SSM_HEADS
    o_gk = o_q + 2 * GLA_KEY_DIM + 2 * GLA_VAL_DIM
    o_gates = o_gk + GLA_GATE_RANK
    w = w.astype(BF16)
    big = jnp.concatenate([w[..., :o_dt], w[..., o_q:o_gk], w[..., o_gates:]], axis=-1)
    pad = jnp.zeros(w.shape[:-1] + (LANES - SSM_HEADS - GLA_GATE_RANK,), w.dtype)
    small = jnp.concatenate([w[..., o_dt:o_q], w[..., o_gk:o_gates], pad], axis=-1)
    return big, small


def _pad_lanes(v, offset):
    v = v.astype(F32)
    return jnp.pad(v, ((0, 0), (offset, LANES - offset - v.shape[-1])))[:, None, :]


def kernel(x, norm_mix_w, w_in, conv_a_w, w_out_a, ssm_conv_w, ssm_conv_b, ssm_dt_bias, ssm_a_log, ssm_d,
           ssm_norm_w, w_out_ssm, gla_w_gk2, gla_b_gk, gla_norm_w, w_out_gla, w_o, norm_mlp_w, w_mlp_up,
           w_mlp_down, norm_f_w):
    b, s, d = x.shape
    m = b * s
    depth = w_in.shape[0]
    x2 = x.reshape(m, d).astype(F32)
    tm_proj = _pick(s, 256)
    ts_mix = _pick(s, 512)
    tm_mlp = _pick(m, 1024)
    rows = lambda v: v.astype(F32)[:, None, :]

    wbig, wsmall = _reorder_w_in(w_in)
    wa, wb, wc, wo = (w.astype(BF16) for w in (w_out_a, w_out_ssm, w_out_gla, w_o))
    wup, wdown = w_mlp_up.astype(BF16), w_mlp_down.astype(BF16)
    nmix, nmlp = rows(norm_mix_w), rows(norm_mlp_w)
    cwa, cwb, cbb = conv_a_w.astype(F32), ssm_conv_w.astype(F32), rows(ssm_conv_b)
    dtb, alog = _pad_lanes(ssm_dt_bias, SMALL_DT_LANE), _pad_lanes(ssm_a_log, SMALL_DT_LANE)
    drow, snw = rows(jnp.repeat(ssm_d, SSM_HEADDIM, axis=-1)), rows(ssm_norm_w)
    wgk = jnp.pad(gla_w_gk2.astype(F32),
                  ((0, 0), (SMALL_GK_LANE, LANES - SMALL_GK_LANE - GLA_GATE_RANK), (0, 0)))
    bgk, gnw = rows(gla_b_gk), rows(gla_norm_w)
    nf = norm_f_w.reshape(1, -1).astype(F32)

    for i in range(depth):
        u, usmall = _proj(x2, nmix[i], wbig[i], wsmall[i], cwa[i], cwb[i], cbb[i], tm=tm_proj, seq=s)
        x2 = _mixer(x2, u, usmall, dtb[i], alog[i], drow[i], snw[i], wgk[i], bgk[i], gnw[i],
                    wa[i], wb[i], wc[i], wo[i], ts=ts_mix, batch=b, seq=s)
        x2 = _mlp(x2, nmlp[i], wup[i], wdown[i], nf, tm=tm_mlp, final_norm=(i == depth - 1))
    return x2.reshape(b, s, d)
```

```python
import functools

import jax
import jax.numpy as jnp
from jax import lax
from jax.experimental import pallas as pl
from jax.experimental.pallas import tpu as pltpu

F32 = jnp.float32
BF16 = jnp.bfloat16

EPS = 1e-6
D_MODEL = 1024
SC_WIDTH = 1024
SC_CONV_WIDTH = 3
SSM_D_INNER = 1024
SSM_HEADDIM = 64
SSM_HEADS = SSM_D_INNER // SSM_HEADDIM
SSM_GROUPS = 4
SSM_HPG = SSM_HEADS // SSM_GROUPS
SSM_STATE = 128
SSM_CONV_WIDTH = 4
SSM_CHUNK = 128
SSM_CONV_DIM = SSM_D_INNER + 2 * SSM_GROUPS * SSM_STATE
SSM_GROUP_WIDTH = SSM_D_INNER // SSM_GROUPS
GLA_HEADS = 4
GLA_KEY_DIM = D_MODEL // 2
GLA_VAL_DIM = D_MODEL
GLA_HEAD_K = GLA_KEY_DIM // GLA_HEADS
GLA_HEAD_V = GLA_VAL_DIM // GLA_HEADS
GLA_GATE_RANK = 16
GLA_GATE_NORMALIZER = 16.0
GLA_CHUNK = 64
N_BRANCHES = 3
D_FF = 4 * D_MODEL

LANES = 128
SUBLANES = 8
VMEM_LIMIT_BYTES = 56 * 1024 * 1024
MXU_COLUMNS = 256
CONV_OUT_BLOCK = MXU_COLUMNS

PROJ_BLOCK = 1024
PB_AX, PB_AB, PB_AC, PB_Z, PB_XBC, PB_QK, PB_V, PB_G, PB_GATES = 0, 1, 2, 3, 4, 6, 7, 8, 9
N_PROJ_BLOCKS = 12
U_VA, U_Z, U_XBC, U_Q, U_K, U_V, U_G, U_GATES = 0, 1024, 2048, 4096, 4608, 5120, 6144, 7168
U_WIDTH = 10240
SMALL_DT_LANE = 0
SMALL_GK_LANE = GLA_GATE_RANK


def _split2(x):
    hi = x.astype(BF16)
    lo = (x - hi.astype(F32)).astype(BF16)
    return hi, lo


def _split3(x):
    hi = x.astype(BF16)
    r = x - hi.astype(F32)
    mid = r.astype(BF16)
    lo = (r - mid.astype(F32)).astype(BF16)
    return hi, mid, lo


def _dot(a, b):
    return jnp.dot(a, b, preferred_element_type=F32)


def _dot_nt(a, b):
    return lax.dot_general(a, b, (((1,), (1,)), ((), ())), preferred_element_type=F32)


def _dot_tn(a, b):
    return lax.dot_general(a, b, (((0,), (0,)), ((), ())), preferred_element_type=F32)


def _dot_exact_rhs(x, m):
    hi, mid, lo = _split3(x)
    return _dot(hi, m) + _dot(mid, m) + _dot(lo, m)


def _dot_exact_lhs(m, x):
    hi, mid, lo = _split3(x)
    return _dot(m, hi) + _dot(m, mid) + _dot(m, lo)


def _silu(x):
    return x * jax.nn.sigmoid(x)


def _softplus(x):
    return jnp.maximum(x, 0.0) + jnp.log1p(jnp.exp(-jnp.abs(x)))


def _log_sigmoid(x):
    return jnp.minimum(x, 0.0) - jnp.log1p(jnp.exp(-jnp.abs(x)))


def _rms(x):
    return x * lax.rsqrt(jnp.mean(x * x, axis=-1, keepdims=True) + EPS)


def _proj_kernel(x_ref, nw_ref, w_ref, wsmall_ref, cwa_ref, cwb_ref, cbb_ref, u_ref, usmall_ref,
                 work_ref, h_ref, *, tiles_per_seq):
    tm = x_ref.shape[0]
    first = (pl.program_id(0) % tiles_per_seq) == 0
    n_conv, n_slabs = work_ref.shape[0], work_ref.shape[1]

    @pl.when(first)
    def _():
        work_ref[:, :, 0:SUBLANES, :] = jnp.zeros((n_conv, n_slabs, SUBLANES, LANES), F32)

    @pl.when(jnp.logical_not(first))
    def _():
        work_ref[:, :, 0:SUBLANES, :] = work_ref[:, :, tm:tm + SUBLANES, :]

    h_ref[...] = (_rms(x_ref[...]) * nw_ref[...]).astype(BF16)
    usmall_ref[...] = _dot(h_ref[...], wsmall_ref[...])

    def proj(blk):
        return _dot(h_ref[...], w_ref[:, blk * PROJ_BLOCK:(blk + 1) * PROJ_BLOCK])

    def causal_conv(slot, pre, cw, width):
        for sl in range(n_slabs):
            work_ref[slot, sl, SUBLANES:SUBLANES + tm, :] = pre[:, sl * LANES:(sl + 1) * LANES]
        y = None
        for k in range(width):
            start = SUBLANES - (width - 1) + k
            shifted = jnp.concatenate(
                [work_ref[slot, sl, pl.ds(start, tm, stride=1), :] for sl in range(n_slabs)], axis=1)
            term = shifted * cw[k:k + 1]
            y = term if y is None else y + term
        return y

    def put(col, val):
        u_ref[:, col:col + PROJ_BLOCK] = val.astype(BF16)

    ax = proj(PB_AX)
    ac = proj(PB_AC)
    conv_a = causal_conv(0, ac * ax, cwa_ref[...], SC_CONV_WIDTH)
    put(U_VA, proj(PB_AB) * conv_a)
    put(U_Z, _silu(proj(PB_Z)))
    cwb = cwb_ref[...]
    cbb = cbb_ref[...]
    for c in range(SSM_CONV_DIM // PROJ_BLOCK):
        cols = slice(c * PROJ_BLOCK, (c + 1) * PROJ_BLOCK)
        y = causal_conv(1 + c, proj(PB_XBC + c), cwb[:, cols], SSM_CONV_WIDTH)
        put(U_XBC + c * PROJ_BLOCK, _silu(y + cbb[:, cols]))
    put(U_Q, proj(PB_QK))
    put(U_V, proj(PB_V))
    put(U_G, _silu(proj(PB_G)))
    for c in range(N_BRANCHES):
        put(U_GATES + c * PROJ_BLOCK, jax.nn.sigmoid(proj(PB_GATES + c)))


def _resident(shape):
    return pl.BlockSpec(shape, lambda *_: (0,) * len(shape), pipeline_mode=pl.Buffered(1))


def _proj(x2, norm_w, wbig, wsmall, conv_a_w, conv_b_w, conv_b_b, *, tm, seq):
    m = x2.shape[0]
    n_conv = 1 + SSM_CONV_DIM // PROJ_BLOCK
    return pl.pallas_call(
        functools.partial(_proj_kernel, tiles_per_seq=seq // tm),
        grid=(m // tm,),
        in_specs=[
            pl.BlockSpec((tm, D_MODEL), lambda i: (i, 0)),
            _resident((1, D_MODEL)),
            _resident((D_MODEL, N_PROJ_BLOCKS * PROJ_BLOCK)),
            _resident((D_MODEL, LANES)),
            _resident((SC_CONV_WIDTH, SC_WIDTH)),
            _resident((SSM_CONV_WIDTH, SSM_CONV_DIM)),
            _resident((1, SSM_CONV_DIM)),
        ],
        out_specs=[
            pl.BlockSpec((tm, U_WIDTH), lambda i: (i, 0)),
            pl.BlockSpec((tm, LANES), lambda i: (i, 0)),
        ],
        out_shape=[
            jax.ShapeDtypeStruct((m, U_WIDTH), BF16),
            jax.ShapeDtypeStruct((m, LANES), F32),
        ],
        scratch_shapes=[pltpu.VMEM((n_conv, PROJ_BLOCK // LANES, tm + SUBLANES, LANES), F32),
                        pltpu.VMEM((tm, D_MODEL), BF16)],
        compiler_params=pltpu.CompilerParams(
            dimension_semantics=("arbitrary",),
            vmem_limit_bytes=VMEM_LIMIT_BYTES),
        name="proj",
    )(x2, norm_w, wbig, wsmall, conv_a_w, conv_b_w, conv_b_b)


def _interleave(streams):
    streams = list(streams)
    while streams:
        for st in list(streams):
            try:
                next(st)
            except StopIteration:
                streams.remove(st)


def _ssd_consts(alog_ref):
    q = SSM_CHUNK
    row = lax.broadcasted_iota(jnp.int32, (q, q), 0)
    colm = lax.broadcasted_iota(jnp.int32, (q, q), 1)
    causal = row >= colm
    ltri = jnp.where(causal, 1.0, 0.0).astype(BF16)
    utri = jnp.where(row <= colm, 1.0, 0.0).astype(BF16)
    hrow = lax.broadcasted_iota(jnp.int32, (2 * LANES, SSM_D_INNER), 0) % LANES
    hcol = lax.broadcasted_iota(jnp.int32, (2 * LANES, SSM_D_INNER), 1)
    expand2 = jnp.where(hcol // SSM_HEADDIM == hrow, 1.0, 0.0).astype(BF16)
    lane = lax.broadcasted_iota(jnp.int32, (1, LANES), 1)
    a_full = jnp.where(lane < SSM_HEADS, -jnp.exp(alog_ref[...]), 0.0)
    glane = lax.broadcasted_iota(jnp.int32, (1, SSM_GROUP_WIDTH), 1)
    return causal, ltri, utri, expand2, a_full, glane


def _ssd_chunk(r0, consts, xbc_ref, zs_ref, small_ref, dtb, drow, nw, state_ref, y_ref):
    q = SSM_CHUNK
    gw = SSM_GROUP_WIDTH
    ns = SSM_STATE
    causal, ltri, utri, expand2, a_full, glane = consts

    dt = _softplus(small_ref[pl.ds(r0, q), :] + dtb)
    dta = dt * a_full
    acol = _dot_exact_lhs(ltri, dta)
    arow = _dot_exact_rhs(dta.T, utri)
    dt_row = dt.T
    yield
    alast = acol[q - 1:q, :]
    ea_cat = jnp.concatenate(_split2(jnp.exp(acol)), axis=1)
    w_cat = jnp.concatenate(_split2(dt * jnp.exp(alast - acol)), axis=1)
    yield

    def expand_group(g):
        gcols = slice(g * gw, (g + 1) * gw)
        return _dot(ea_cat, expand2[:, gcols]), _dot(w_cat, expand2[:, gcols])

    nxt = expand_group(0)
    for g in range(SSM_GROUPS):
        gcols = slice(g * gw, (g + 1) * gw)
        ea_g, w_g = nxt
        if g + 1 < SSM_GROUPS:
            nxt = expand_group(g + 1)
        xs_g = xbc_ref[pl.ds(r0, q), gcols]
        xs_f = xs_g.astype(F32)
        bm_g = xbc_ref[pl.ds(r0, q), SSM_D_INNER + g * ns:SSM_D_INNER + (g + 1) * ns]
        cm_g = xbc_ref[pl.ds(r0, q), SSM_D_INNER + (SSM_GROUPS + g) * ns:
                       SSM_D_INNER + (SSM_GROUPS + g + 1) * ns]
        cbm = _dot_nt(cm_g, bm_g)
        st = state_ref[g]
        y_off = _dot(cm_g, st.astype(BF16)) * ea_g
        state_ref[g] = st * ea_g[q - 1:q, :] + _dot_tn(bm_g, (xs_f * w_g).astype(BF16))
        yield
        lhs, rhs = [], []
        for r in range(SSM_HPG):
            h = g * SSM_HPG + r
            seg = acol[:, h:h + 1] - arow[h:h + 1, :]
            dec = jnp.where(causal, jnp.exp(seg), 0.0) * dt_row[h:h + 1, :]
            lhs.append((cbm * dec).astype(BF16))
            rhs.append(jnp.where(glane // SSM_HEADDIM == r, xs_g, jnp.zeros_like(xs_g)))
        y_diag = _dot(jnp.concatenate(lhs, axis=1), jnp.concatenate(rhs, axis=0))
        yield
        yg = y_diag + y_off + xs_f * drow[:, gcols]
        yg = _rms(yg * zs_ref[pl.ds(r0, q), gcols].astype(F32))
        y_ref[pl.ds(r0, q), gcols] = (yg * nw[:, gcols]).astype(BF16)


def _gla_chunk(r0, causal, ltri, q_ref, k_ref, v_ref, gs_ref, gk_ref, nw, state_ref, o_ref):
    t = GLA_CHUNK
    dk, dv = GLA_HEAD_K, GLA_HEAD_V
    scale = dk ** -0.5
    gcum = _dot_exact_lhs(ltri, gk_ref[pl.ds(r0, t), :])
    yield
    glast = gcum[t - 1:t, :]
    qf = q_ref[pl.ds(r0, t), :].astype(F32)
    kf = k_ref[pl.ds(r0, t), :].astype(F32)
    q_in = (qf * scale * jnp.exp(gcum)).astype(BF16)
    k_in = (kf * jnp.exp(-gcum)).astype(BF16)
    k_end = (kf * jnp.exp(glast - gcum)).astype(BF16)
    cdec = jnp.exp(glast)
    for h in range(GLA_HEADS):
        kcols = slice(h * dk, (h + 1) * dk)
        vcols = slice(h * dv, (h + 1) * dv)
        qh = q_in[:, kcols]
        vh = v_ref[pl.ds(r0, t), vcols]
        sc = _dot_nt(qh, k_in[:, kcols])
        st = state_ref[h]
        o_inter = _dot_nt(qh, st.astype(BF16))
        state_ref[h] = st * cdec[:, kcols] + _dot_tn(vh, k_end[:, kcols])
        yield
        o = _dot(jnp.where(causal, sc, 0.0).astype(BF16), vh) + o_inter
        yield
        o = _rms(o) * nw * gs_ref[pl.ds(r0, t), vcols].astype(F32)
        o_ref[pl.ds(r0, t), vcols] = o.astype(BF16)


def _mixer_kernel(x_ref, va_ref, zs_ref, xbc_ref, q_ref, k_ref, v_ref, gs_ref, ga_ref, gb_ref, gc_ref,
                  small_ref, dtb_ref, alog_ref, drow_ref, snw_ref, wgk_ref, bgk_ref, gnw_ref,
                  wa_ref, wb_ref, wc_ref, wo_ref, out_ref, sstate_ref, gstate_ref, y_ref, o_ref, gk_ref, ya_ref):
    ts = x_ref.shape[0]

    @pl.when(pl.program_id(1) == 0)
    def _():
        sstate_ref[...] = jnp.zeros_like(sstate_ref)
        gstate_ref[...] = jnp.zeros_like(gstate_ref)

    w_hi, w_lo = _split2(wgk_ref[...])
    s_hi, s_lo = _split2(small_ref[...])
    zg = _dot(s_hi, w_hi) + _dot(s_hi, w_lo) + _dot(s_lo, w_hi) + bgk_ref[...]
    gk_ref[...] = _log_sigmoid(zg) / GLA_GATE_NORMALIZER

    consts = _ssd_consts(alog_ref)
    t = GLA_CHUNK
    grow = lax.broadcasted_iota(jnp.int32, (t, t), 0)
    gcol = lax.broadcasted_iota(jnp.int32, (t, t), 1)
    gcausal = grow >= gcol
    gltri = jnp.where(gcausal, 1.0, 0.0).astype(BF16)
    dtb = dtb_ref[...]
    drow = drow_ref[...]
    snw = snw_ref[...]
    gnw = gnw_ref[...]
    gla_per_ssd = SSM_CHUNK // GLA_CHUNK

    def step(c, carry):
        r0 = pl.multiple_of(c * SSM_CHUNK, SSM_CHUNK)

        def gla_stream():
            for sub in range(gla_per_ssd):
                yield from _gla_chunk(pl.multiple_of(r0 + sub * t, t), gcausal, gltri, q_ref, k_ref, v_ref,
                                      gs_ref, gk_ref, gnw, gstate_ref, o_ref)

        def conv_out_stream():
            for cb in range(D_MODEL // CONV_OUT_BLOCK):
                ccols = slice(cb * CONV_OUT_BLOCK, (cb + 1) * CONV_OUT_BLOCK)
                ya_ref[pl.ds(r0, SSM_CHUNK), ccols] = (
                    ga_ref[pl.ds(r0, SSM_CHUNK), ccols].astype(F32)
                    * _dot(va_ref[pl.ds(r0, SSM_CHUNK), :], wa_ref[:, ccols]))
                yield

        _interleave([_ssd_chunk(r0, consts, xbc_ref, zs_ref, small_ref, dtb, drow, snw, sstate_ref, y_ref),
                     gla_stream(), conv_out_stream()])
        return carry

    lax.fori_loop(0, ts // SSM_CHUNK, step, 0)

    merged = (ya_ref[...]
              + gb_ref[...].astype(F32) * _dot(y_ref[...], wb_ref[...])
              + gc_ref[...].astype(F32) * _dot(o_ref[...], wc_ref[...]))
    out_ref[...] = x_ref[...] + _dot(merged.astype(BF16), wo_ref[...])


def _mixer(x2, u, usmall, dt_bias, a_log, d_row, ssm_nw, wgk_full, b_gk, gla_nw, wa, wb, wc, wo,
           *, ts, batch, seq):
    m = x2.shape[0]
    tps = seq // ts

    def rows(w, c):
        return pl.BlockSpec((ts, w), lambda b, j: (b * tps + j, c // w))

    return pl.pallas_call(
        _mixer_kernel,
        grid=(batch, tps),
        in_specs=[
            rows(D_MODEL, 0),
            rows(SC_WIDTH, U_VA),
            rows(SSM_D_INNER, U_Z),
            rows(SSM_CONV_DIM, U_XBC),
            rows(GLA_KEY_DIM, U_Q),
            rows(GLA_KEY_DIM, U_K),
            rows(GLA_VAL_DIM, U_V),
            rows(GLA_VAL_DIM, U_G),
            rows(D_MODEL, U_GATES),
            rows(D_MODEL, U_GATES + D_MODEL),
            rows(D_MODEL, U_GATES + 2 * D_MODEL),
            rows(LANES, 0),
            _resident((1, LANES)),
            _resident((1, LANES)),
            _resident((1, SSM_D_INNER)),
            _resident((1, SSM_D_INNER)),
            _resident((LANES, GLA_KEY_DIM)),
            _resident((1, GLA_KEY_DIM)),
            _resident((1, GLA_HEAD_V)),
            _resident((SC_WIDTH, D_MODEL)),
            _resident((SSM_D_INNER, D_MODEL)),
            _resident((GLA_VAL_DIM, D_MODEL)),
            _resident((D_MODEL, D_MODEL)),
        ],
        out_specs=pl.BlockSpec((ts, D_MODEL), lambda b, j: (b * tps + j, 0)),
        out_shape=jax.ShapeDtypeStruct((m, D_MODEL), F32),
        scratch_shapes=[
            pltpu.VMEM((SSM_GROUPS, SSM_STATE, SSM_GROUP_WIDTH), F32),
            pltpu.VMEM((GLA_HEADS, GLA_HEAD_V, GLA_HEAD_K), F32),
            pltpu.VMEM((ts, SSM_D_INNER), BF16),
            pltpu.VMEM((ts, GLA_VAL_DIM), BF16),
            pltpu.VMEM((ts, GLA_KEY_DIM), F32),
            pltpu.VMEM((ts, D_MODEL), F32),
        ],
        compiler_params=pltpu.CompilerParams(
            dimension_semantics=("arbitrary", "arbitrary"),
            vmem_limit_bytes=VMEM_LIMIT_BYTES),
        name="mixer",
    )(x2, u, u, u, u, u, u, u, u, u, u, usmall, dt_bias, a_log, d_row, ssm_nw, wgk_full, b_gk, gla_nw,
      wa, wb, wc, wo)


def _mlp_kernel(x_ref, nw_ref, wup_ref, wdown_ref, nf_ref, out_ref, *, ff_chunk, final_norm):
    x = x_ref[...]
    hb = (_rms(x) * nw_ref[...]).astype(BF16)
    acc = x
    for c in range(D_FF // ff_chunk):
        up = _dot(hb, wup_ref[:, c * ff_chunk:(c + 1) * ff_chunk])
        act = jnp.square(jnp.maximum(up, 0.0)).astype(BF16)
        acc = acc + _dot(act, wdown_ref[c * ff_chunk:(c + 1) * ff_chunk, :])
    if final_norm:
        acc = _rms(acc) * nf_ref[...]
    out_ref[...] = acc


def _mlp(x2, norm_w, wup, wdown, norm_f, *, tm, final_norm):
    m = x2.shape[0]
    return pl.pallas_call(
        functools.partial(_mlp_kernel, ff_chunk=1024, final_norm=final_norm),
        grid=(m // tm,),
        in_specs=[
            pl.BlockSpec((tm, D_MODEL), lambda i: (i, 0)),
            _resident((1, D_MODEL)),
            _resident((D_MODEL, D_FF)),
            _resident((D_FF, D_MODEL)),
            _resident((1, D_MODEL)),
        ],
        out_specs=pl.BlockSpec((tm, D_MODEL), lambda i: (i, 0)),
        out_shape=jax.ShapeDtypeStruct((m, D_MODEL), F32),
        compiler_params=pltpu.CompilerParams(
            dimension_semantics=("arbitrary",),
            vmem_limit_bytes=VMEM_LIMIT_BYTES),
        name="mlp",
    )(x2, norm_w, wup, wdown, norm_f)


def _pick(n, pref):
    t = min(pref, n)
    while n % t:
        t //= 2
    return t


def _reorder_w_in(w):
    o_dt = 3 * SC_WIDTH + SSM_D_INNER + SSM_CONV_DIM
    o_q = o_dt + SSM_HEADS
    o_gk = o_q + 2 * GLA_KEY_DIM + 2 * GLA_VAL_DIM
    o_gates = o_gk + GLA_GATE_RANK
    w = w.astype(BF16)
    big = jnp.concatenate([w[..., :o_dt], w[..., o_q:o_gk], w[..., o_gates:]], axis=-1)
    pad = jnp.zeros(w.shape[:-1] + (LANES - SSM_HEADS - GLA_GATE_RANK,), w.dtype)
    small = jnp.concatenate([w[..., o_dt:o_q], w[..., o_gk:o_gates], pad], axis=-1)
    return big, small


def _pad_lanes(v, offset):
    v = v.astype(F32)
    return jnp.pad(v, ((0, 0), (offset, LANES - offset - v.shape[-1])))[:, None, :]


def kernel(x, norm_mix_w, w_in, conv_a_w, w_out_a, ssm_conv_w, ssm_conv_b, ssm_dt_bias, ssm_a_log, ssm_d,
           ssm_norm_w, w_out_ssm, gla_w_gk2, gla_b_gk, gla_norm_w, w_out_gla, w_o, norm_mlp_w, w_mlp_up,
           w_mlp_down, norm_f_w):
    b, s, d = x.shape
    m = b * s
    depth = w_in.shape[0]
    x2 = x.reshape(m, d).astype(F32)
    tm_proj = _pick(s, 256)
    ts_mix = _pick(s, 512)
    tm_mlp = _pick(m, 1024)
    rows = lambda v: v.astype(F32)[:, None, :]

    wbig, wsmall = _reorder_w_in(w_in)
    wa, wb, wc, wo = (w.astype(BF16) for w in (w_out_a, w_out_ssm, w_out_gla, w_o))
    wup, wdown = w_mlp_up.astype(BF16), w_mlp_down.astype(BF16)
    nmix, nmlp = rows(norm_mix_w), rows(norm_mlp_w)
    cwa, cwb, cbb = conv_a_w.astype(F32), ssm_conv_w.astype(F32), rows(ssm_conv_b)
    dtb, alog = _pad_lanes(ssm_dt_bias, SMALL_DT_LANE), _pad_lanes(ssm_a_log, SMALL_DT_LANE)
    drow, snw = rows(jnp.repeat(ssm_d, SSM_HEADDIM, axis=-1)), rows(ssm_norm_w)
    wgk = jnp.pad(gla_w_gk2.astype(F32),
                  ((0, 0), (SMALL_GK_LANE, LANES - SMALL_GK_LANE - GLA_GATE_RANK), (0, 0)))
    bgk, gnw = rows(gla_b_gk), rows(gla_norm_w)
    nf = norm_f_w.reshape(1, -1).astype(F32)

    for i in range(depth):
        u, usmall = _proj(x2, nmix[i], wbig[i], wsmall[i], cwa[i], cwb[i], cbb[i], tm=tm_proj, seq=s)
        x2 = _mixer(x2, u, usmall, dtb[i], alog[i], drow[i], snw[i], wgk[i], bgk[i], gnw[i],
                    wa[i], wb[i], wc[i], wo[i], ts=ts_mix, batch=b, seq=s)
        x2 = _mlp(x2, nmlp[i], wup[i], wdown[i], nf, tm=tm_mlp, final_norm=(i == depth - 1))
    return x2.reshape(b, s, d)
```

```python
import functools

import jax
import jax.numpy as jnp
from jax import lax
from jax.experimental import pallas as pl
from jax.experimental.pallas import tpu as pltpu

F32 = jnp.float32
BF16 = jnp.bfloat16

EPS = 1e-6
D_MODEL = 1024
SC_WIDTH = 1024
SC_CONV_WIDTH = 3
SSM_D_INNER = 1024
SSM_HEADDIM = 64
SSM_HEADS = SSM_D_INNER // SSM_HEADDIM
SSM_GROUPS = 4
SSM_HPG = SSM_HEADS // SSM_GROUPS
SSM_STATE = 128
SSM_CONV_WIDTH = 4
SSM_CHUNK = 128
SSM_CONV_DIM = SSM_D_INNER + 2 * SSM_GROUPS * SSM_STATE
SSM_GROUP_WIDTH = SSM_D_INNER // SSM_GROUPS
GLA_HEADS = 4
GLA_KEY_DIM = D_MODEL // 2
GLA_VAL_DIM = D_MODEL
GLA_HEAD_K = GLA_KEY_DIM // GLA_HEADS
GLA_HEAD_V = GLA_VAL_DIM // GLA_HEADS
GLA_GATE_RANK = 16
GLA_GATE_NORMALIZER = 16.0
GLA_CHUNK = 64
N_BRANCHES = 3
D_FF = 4 * D_MODEL

LANES = 128
SUBLANES = 8
VMEM_LIMIT_BYTES = 60 * 1024 * 1024

PROJ_BLOCK = 1024
PB_AX, PB_AB, PB_AC, PB_Z, PB_XBC, PB_QK, PB_V, PB_G, PB_GATES = 0, 1, 2, 3, 4, 6, 7, 8, 9
N_PROJ_BLOCKS = 12
U_VA, U_Z, U_XBC, U_Q, U_K, U_V, U_G, U_GATES = 0, 1024, 2048, 4096, 4608, 5120, 6144, 7168
U_WIDTH = 10240
SMALL_DT_LANE = 0
SMALL_GK_LANE = GLA_GATE_RANK


def _split2(x):
    hi = x.astype(BF16)
    lo = (x - hi.astype(F32)).astype(BF16)
    return hi, lo


def _split3(x):
    hi = x.astype(BF16)
    r = x - hi.astype(F32)
    mid = r.astype(BF16)
    lo = (r - mid.astype(F32)).astype(BF16)
    return hi, mid, lo


def _dot(a, b):
    return jnp.dot(a, b, preferred_element_type=F32)


def _dot_nt(a, b):
    return lax.dot_general(a, b, (((1,), (1,)), ((), ())), preferred_element_type=F32)


def _dot_tn(a, b):
    return lax.dot_general(a, b, (((0,), (0,)), ((), ())), preferred_element_type=F32)


def _dot_exact_rhs(x, m):
    hi, mid, lo = _split3(x)
    return _dot(hi, m) + _dot(mid, m) + _dot(lo, m)


def _dot_exact_lhs(m, x):
    hi, mid, lo = _split3(x)
    return _dot(m, hi) + _dot(m, mid) + _dot(m, lo)


def _silu(x):
    return x * jax.nn.sigmoid(x)


def _softplus(x):
    return jnp.maximum(x, 0.0) + jnp.log1p(jnp.exp(-jnp.abs(x)))


def _log_sigmoid(x):
    return jnp.minimum(x, 0.0) - jnp.log1p(jnp.exp(-jnp.abs(x)))


def _rms(x):
    return x * lax.rsqrt(jnp.mean(x * x, axis=-1, keepdims=True) + EPS)


def _proj_kernel(x_ref, nw_ref, w_ref, wsmall_ref, cwa_ref, cwb_ref, cbb_ref, u_ref, usmall_ref,
                 work_ref, h_ref, *, tiles_per_seq):
    tm = x_ref.shape[0]
    first = (pl.program_id(0) % tiles_per_seq) == 0
    n_conv, n_slabs = work_ref.shape[0], work_ref.shape[1]

    @pl.when(first)
    def _():
        work_ref[:, :, 0:SUBLANES, :] = jnp.zeros((n_conv, n_slabs, SUBLANES, LANES), F32)

    @pl.when(jnp.logical_not(first))
    def _():
        work_ref[:, :, 0:SUBLANES, :] = work_ref[:, :, tm:tm + SUBLANES, :]

    h_ref[...] = (_rms(x_ref[...]) * nw_ref[...]).astype(BF16)
    usmall_ref[...] = _dot(h_ref[...], wsmall_ref[...])

    def proj(blk):
        return _dot(h_ref[...], w_ref[:, blk * PROJ_BLOCK:(blk + 1) * PROJ_BLOCK])

    def causal_conv(slot, pre, cw, width):
        for sl in range(n_slabs):
            work_ref[slot, sl, SUBLANES:SUBLANES + tm, :] = pre[:, sl * LANES:(sl + 1) * LANES]
        y = None
        for k in range(width):
            start = SUBLANES - (width - 1) + k
            shifted = jnp.concatenate(
                [work_ref[slot, sl, pl.ds(start, tm, stride=1), :] for sl in range(n_slabs)], axis=1)
            term = shifted * cw[k:k + 1]
            y = term if y is None else y + term
        return y

    def put(col, val):
        u_ref[:, col:col + PROJ_BLOCK] = val.astype(BF16)

    ax = proj(PB_AX)
    ac = proj(PB_AC)
    conv_a = causal_conv(0, ac * ax, cwa_ref[...], SC_CONV_WIDTH)
    put(U_VA, proj(PB_AB) * conv_a)
    put(U_Z, _silu(proj(PB_Z)))
    cwb = cwb_ref[...]
    cbb = cbb_ref[...]
    for c in range(SSM_CONV_DIM // PROJ_BLOCK):
        cols = slice(c * PROJ_BLOCK, (c + 1) * PROJ_BLOCK)
        y = causal_conv(1 + c, proj(PB_XBC + c), cwb[:, cols], SSM_CONV_WIDTH)
        put(U_XBC + c * PROJ_BLOCK, _silu(y + cbb[:, cols]))
    put(U_Q, proj(PB_QK))
    put(U_V, proj(PB_V))
    put(U_G, _silu(proj(PB_G)))
    for c in range(N_BRANCHES):
        put(U_GATES + c * PROJ_BLOCK, jax.nn.sigmoid(proj(PB_GATES + c)))


def _resident(shape):
    return pl.BlockSpec(shape, lambda *_: (0,) * len(shape), pipeline_mode=pl.Buffered(1))


def _proj(x2, norm_w, wbig, wsmall, conv_a_w, conv_b_w, conv_b_b, *, tm, seq):
    m = x2.shape[0]
    n_conv = 1 + SSM_CONV_DIM // PROJ_BLOCK
    return pl.pallas_call(
        functools.partial(_proj_kernel, tiles_per_seq=seq // tm),
        grid=(m // tm,),
        in_specs=[
            pl.BlockSpec((tm, D_MODEL), lambda i: (i, 0)),
            _resident((1, D_MODEL)),
            _resident((D_MODEL, N_PROJ_BLOCKS * PROJ_BLOCK)),
            _resident((D_MODEL, LANES)),
            _resident((SC_CONV_WIDTH, SC_WIDTH)),
            _resident((SSM_CONV_WIDTH, SSM_CONV_DIM)),
            _resident((1, SSM_CONV_DIM)),
        ],
        out_specs=[
            pl.BlockSpec((tm, U_WIDTH), lambda i: (i, 0)),
            pl.BlockSpec((tm, LANES), lambda i: (i, 0)),
        ],
        out_shape=[
            jax.ShapeDtypeStruct((m, U_WIDTH), BF16),
            jax.ShapeDtypeStruct((m, LANES), F32),
        ],
        scratch_shapes=[pltpu.VMEM((n_conv, PROJ_BLOCK // LANES, tm + SUBLANES, LANES), F32),
                        pltpu.VMEM((tm, D_MODEL), BF16)],
        compiler_params=pltpu.CompilerParams(
            dimension_semantics=("arbitrary",),
            vmem_limit_bytes=VMEM_LIMIT_BYTES),
        name="proj",
    )(x2, norm_w, wbig, wsmall, conv_a_w, conv_b_w, conv_b_b)


def _interleave(streams):
    streams = list(streams)
    while streams:
        for st in list(streams):
            try:
                next(st)
            except StopIteration:
                streams.remove(st)


def _ssd_consts(alog_ref):
    q = SSM_CHUNK
    row = lax.broadcasted_iota(jnp.int32, (q, q), 0)
    colm = lax.broadcasted_iota(jnp.int32, (q, q), 1)
    causal = row >= colm
    ltri = jnp.where(causal, 1.0, 0.0).astype(BF16)
    utri = jnp.where(row <= colm, 1.0, 0.0).astype(BF16)
    hrow = lax.broadcasted_iota(jnp.int32, (2 * LANES, SSM_D_INNER), 0) % LANES
    hcol = lax.broadcasted_iota(jnp.int32, (2 * LANES, SSM_D_INNER), 1)
    expand2 = jnp.where(hcol // SSM_HEADDIM == hrow, 1.0, 0.0).astype(BF16)
    lane = lax.broadcasted_iota(jnp.int32, (1, LANES), 1)
    a_full = jnp.where(lane < SSM_HEADS, -jnp.exp(alog_ref[...]), 0.0)
    glane = lax.broadcasted_iota(jnp.int32, (1, SSM_GROUP_WIDTH), 1)
    return causal, ltri, utri, expand2, a_full, glane


def _ssd_chunk(r0, consts, xbc_ref, zs_ref, small_ref, dtb, drow, nw, state_ref, y_ref):
    q = SSM_CHUNK
    gw = SSM_GROUP_WIDTH
    ns = SSM_STATE
    causal, ltri, utri, expand2, a_full, glane = consts

    dt = _softplus(small_ref[pl.ds(r0, q), :] + dtb)
    dta = dt * a_full
    acol = _dot_exact_lhs(ltri, dta)
    arow = _dot_exact_rhs(dta.T, utri)
    dt_row = dt.T
    yield
    alast = acol[q - 1:q, :]
    ea_cat = jnp.concatenate(_split2(jnp.exp(acol)), axis=1)
    w_cat = jnp.concatenate(_split2(dt * jnp.exp(alast - acol)), axis=1)
    yield

    def expand_group(g):
        gcols = slice(g * gw, (g + 1) * gw)
        return _dot(ea_cat, expand2[:, gcols]), _dot(w_cat, expand2[:, gcols])

    nxt = expand_group(0)
    for g in range(SSM_GROUPS):
        gcols = slice(g * gw, (g + 1) * gw)
        ea_g, w_g = nxt
        if g + 1 < SSM_GROUPS:
            nxt = expand_group(g + 1)
        xs_g = xbc_ref[pl.ds(r0, q), gcols]
        xs_f = xs_g.astype(F32)
        bm_g = xbc_ref[pl.ds(r0, q), SSM_D_INNER + g * ns:SSM_D_INNER + (g + 1) * ns]
        cm_g = xbc_ref[pl.ds(r0, q), SSM_D_INNER + (SSM_GROUPS + g) * ns:
                       SSM_D_INNER + (SSM_GROUPS + g + 1) * ns]
        cbm = _dot_nt(cm_g, bm_g)
        st = state_ref[g]
        y_off = _dot(cm_g, st.astype(BF16)) * ea_g
        state_ref[g] = st * ea_g[q - 1:q, :] + _dot_tn(bm_g, (xs_f * w_g).astype(BF16))
        yield
        lhs, rhs = [], []
        for r in range(SSM_HPG):
            h = g * SSM_HPG + r
            seg = acol[:, h:h + 1] - arow[h:h + 1, :]
            dec = jnp.where(causal, jnp.exp(seg), 0.0) * dt_row[h:h + 1, :]
            lhs.append((cbm * dec).astype(BF16))
            rhs.append(jnp.where(glane // SSM_HEADDIM == r, xs_g, jnp.zeros_like(xs_g)))
        y_diag = _dot(jnp.concatenate(lhs, axis=1), jnp.concatenate(rhs, axis=0))
        yield
        yg = y_diag + y_off + xs_f * drow[:, gcols]
        yg = _rms(yg * zs_ref[pl.ds(r0, q), gcols].astype(F32))
        y_ref[pl.ds(r0, q), gcols] = (yg * nw[:, gcols]).astype(BF16)


def _gla_chunk(r0, causal, ltri, q_ref, k_ref, v_ref, gs_ref, gk_ref, nw, state_ref, o_ref):
    t = GLA_CHUNK
    dk, dv = GLA_HEAD_K, GLA_HEAD_V
    scale = dk ** -0.5
    gcum = _dot_exact_lhs(ltri, gk_ref[pl.ds(r0, t), :])
    yield
    glast = gcum[t - 1:t, :]
    qf = q_ref[pl.ds(r0, t), :].astype(F32)
    kf = k_ref[pl.ds(r0, t), :].astype(F32)
    q_in = (qf * scale * jnp.exp(gcum)).astype(BF16)
    k_in = (kf * jnp.exp(-gcum)).astype(BF16)
    k_end = (kf * jnp.exp(glast - gcum)).astype(BF16)
    cdec = jnp.exp(glast)
    for h in range(GLA_HEADS):
        kcols = slice(h * dk, (h + 1) * dk)
        vcols = slice(h * dv, (h + 1) * dv)
        qh = q_in[:, kcols]
        vh = v_ref[pl.ds(r0, t), vcols]
        sc = _dot_nt(qh, k_in[:, kcols])
        st = state_ref[h]
        o_inter = _dot_nt(qh, st.astype(BF16))
        state_ref[h] = st * cdec[:, kcols] + _dot_tn(vh, k_end[:, kcols])
        yield
        o = _dot(jnp.where(causal, sc, 0.0).astype(BF16), vh) + o_inter
        yield
        o = _rms(o) * nw * gs_ref[pl.ds(r0, t), vcols].astype(F32)
        o_ref[pl.ds(r0, t), vcols] = o.astype(BF16)


def _mixer_kernel(x_ref, va_ref, zs_ref, xbc_ref, q_ref, k_ref, v_ref, gs_ref, ga_ref, gb_ref, gc_ref,
                  small_ref, dtb_ref, alog_ref, drow_ref, snw_ref, wgk_ref, bgk_ref, gnw_ref,
                  wa_ref, wb_ref, wc_ref, wo_ref, out_ref, sstate_ref, gstate_ref, y_ref, o_ref, gk_ref):
    ts = x_ref.shape[0]

    @pl.when(pl.program_id(1) == 0)
    def _():
        sstate_ref[...] = jnp.zeros_like(sstate_ref)
        gstate_ref[...] = jnp.zeros_like(gstate_ref)

    w_hi, w_lo = _split2(wgk_ref[...])
    s_hi, s_lo = _split2(small_ref[...])
    zg = _dot(s_hi, w_hi) + _dot(s_hi, w_lo) + _dot(s_lo, w_hi) + bgk_ref[...]
    gk_ref[...] = _log_sigmoid(zg) / GLA_GATE_NORMALIZER

    consts = _ssd_consts(alog_ref)
    t = GLA_CHUNK
    grow = lax.broadcasted_iota(jnp.int32, (t, t), 0)
    gcol = lax.broadcasted_iota(jnp.int32, (t, t), 1)
    gcausal = grow >= gcol
    gltri = jnp.where(gcausal, 1.0, 0.0).astype(BF16)
    dtb = dtb_ref[...]
    drow = drow_ref[...]
    snw = snw_ref[...]
    gnw = gnw_ref[...]
    gla_per_ssd = SSM_CHUNK // GLA_CHUNK

    def step(c, carry):
        r0 = pl.multiple_of(c * SSM_CHUNK, SSM_CHUNK)

        def gla_stream():
            for sub in range(gla_per_ssd):
                yield from _gla_chunk(pl.multiple_of(r0 + sub * t, t), gcausal, gltri, q_ref, k_ref, v_ref,
                                      gs_ref, gk_ref, gnw, gstate_ref, o_ref)

        _interleave([_ssd_chunk(r0, consts, xbc_ref, zs_ref, small_ref, dtb, drow, snw, sstate_ref, y_ref),
                     gla_stream()])
        return carry

    lax.fori_loop(0, ts // SSM_CHUNK, step, 0)

    merged = (ga_ref[...].astype(F32) * _dot(va_ref[...], wa_ref[...])
              + gb_ref[...].astype(F32) * _dot(y_ref[...], wb_ref[...])
              + gc_ref[...].astype(F32) * _dot(o_ref[...], wc_ref[...]))
    out_ref[...] = x_ref[...] + _dot(merged.astype(BF16), wo_ref[...])


def _mixer(x2, u, usmall, dt_bias, a_log, d_row, ssm_nw, wgk_full, b_gk, gla_nw, wa, wb, wc, wo,
           *, ts, batch, seq):
    m = x2.shape[0]
    tps = seq // ts

    def rows(w, c):
        return pl.BlockSpec((ts, w), lambda b, j: (b * tps + j, c // w))

    return pl.pallas_call(
        _mixer_kernel,
        grid=(batch, tps),
        in_specs=[
            rows(D_MODEL, 0),
            rows(SC_WIDTH, U_VA),
            rows(SSM_D_INNER, U_Z),
            rows(SSM_CONV_DIM, U_XBC),
            rows(GLA_KEY_DIM, U_Q),
            rows(GLA_KEY_DIM, U_K),
            rows(GLA_VAL_DIM, U_V),
            rows(GLA_VAL_DIM, U_G),
            rows(D_MODEL, U_GATES),
            rows(D_MODEL, U_GATES + D_MODEL),
            rows(D_MODEL, U_GATES + 2 * D_MODEL),
            rows(LANES, 0),
            _resident((1, LANES)),
            _resident((1, LANES)),
            _resident((1, SSM_D_INNER)),
            _resident((1, SSM_D_INNER)),
            _resident((LANES, GLA_KEY_DIM)),
            _resident((1, GLA_KEY_DIM)),
            _resident((1, GLA_HEAD_V)),
            _resident((SC_WIDTH, D_MODEL)),
            _resident((SSM_D_INNER, D_MODEL)),
            _resident((GLA_VAL_DIM, D_MODEL)),
            _resident((D_MODEL, D_MODEL)),
        ],
        out_specs=pl.BlockSpec((ts, D_MODEL), lambda b, j: (b * tps + j, 0)),
        out_shape=jax.ShapeDtypeStruct((m, D_MODEL), F32),
        scratch_shapes=[
            pltpu.VMEM((SSM_GROUPS, SSM_STATE, SSM_GROUP_WIDTH), F32),
            pltpu.VMEM((GLA_HEADS, GLA_HEAD_V, GLA_HEAD_K), F32),
            pltpu.VMEM((ts, SSM_D_INNER), BF16),
            pltpu.VMEM((ts, GLA_VAL_DIM), BF16),
            pltpu.VMEM((ts, GLA_KEY_DIM), F32),
        ],
        compiler_params=pltpu.CompilerParams(
            dimension_semantics=("arbitrary", "arbitrary"),
            vmem_limit_bytes=VMEM_LIMIT_BYTES),
        name="mixer",
    )(x2, u, u, u, u, u, u, u, u, u, u, usmall, dt_bias, a_log, d_row, ssm_nw, wgk_full, b_gk, gla_nw,
      wa, wb, wc, wo)


def _mlp_kernel(x_ref, nw_ref, wup_ref, wdown_ref, nf_ref, out_ref, *, ff_chunk, final_norm):
    x = x_ref[...]
    hb = (_rms(x) * nw_ref[...]).astype(BF16)
    acc = x
    for c in range(D_FF // ff_chunk):
        up = _dot(hb, wup_ref[:, c * ff_chunk:(c + 1) * ff_chunk])
        act = jnp.square(jnp.maximum(up, 0.0)).astype(BF16)
        acc = acc + _dot(act, wdown_ref[c * ff_chunk:(c + 1) * ff_chunk, :])
    if final_norm:
        acc = _rms(acc) * nf_ref[...]
    out_ref[...] = acc


def _mlp(x2, norm_w, wup, wdown, norm_f, *, tm, final_norm):
    m = x2.shape[0]
    return pl.pallas_call(
        functools.partial(_mlp_kernel, ff_chunk=1024, final_norm=final_norm),
        grid=(m // tm,),
        in_specs=[
            pl.BlockSpec((tm, D_MODEL), lambda i: (i, 0)),
            _resident((1, D_MODEL)),
            _resident((D_MODEL, D_FF)),
            _resident((D_FF, D_MODEL)),
            _resident((1, D_MODEL)),
        ],
        out_specs=pl.BlockSpec((tm, D_MODEL), lambda i: (i, 0)),
        out_shape=jax.ShapeDtypeStruct((m, D_MODEL), F32),
        compiler_params=pltpu.CompilerParams(
            dimension_semantics=("arbitrary",),
            vmem_limit_bytes=VMEM_LIMIT_BYTES),
        name="mlp",
    )(x2, norm_w, wup, wdown, norm_f)


def _pick(n, pref):
    t = min(pref, n)
    while n % t:
        t //= 2
    return t


def _reorder_w_in(w):
    o_dt = 3 * SC_WIDTH + SSM_D_INNER + SSM_CONV_DIM
    o_q = o_dt + SSM_HEADS
    o_gk = o_q + 2 * GLA_KEY_DIM + 2 * GLA_VAL_DIM
    o_gates = o_gk + GLA_GATE_RANK
    w = w.astype(BF16)
    big = jnp.concatenate([w[..., :o_dt], w[..., o_q:o_gk], w[..., o_gates:]], axis=-1)
    pad = jnp.zeros(w.shape[:-1] + (LANES - SSM_HEADS - GLA_GATE_RANK,), w.dtype)
    small = jnp.concatenate([w[..., o_dt:o_q], w[..., o_gk:o_gates], pad], axis=-1)
    return big, small


def _pad_lanes(v, offset):
    v = v.astype(F32)
    return jnp.pad(v, ((0, 0), (offset, LANES - offset - v.shape[-1])))[:, None, :]


def kernel(x, norm_mix_w, w_in, conv_a_w, w_out_a, ssm_conv_w, ssm_conv_b, ssm_dt_bias, ssm_a_log, ssm_d,
           ssm_norm_w, w_out_ssm, gla_w_gk2, gla_b_gk, gla_norm_w, w_out_gla, w_o, norm_mlp_w, w_mlp_up,
           w_mlp_down, norm_f_w):
    b, s, d = x.shape
    m = b * s
    depth = w_in.shape[0]
    x2 = x.reshape(m, d).astype(F32)
    tm_proj = _pick(s, 512)
    ts_mix = _pick(s, 512)
    tm_mlp = _pick(m, 1024)
    rows = lambda v: v.astype(F32)[:, None, :]

    wbig, wsmall = _reorder_w_in(w_in)
    wa, wb, wc, wo = (w.astype(BF16) for w in (w_out_a, w_out_ssm, w_out_gla, w_o))
    wup, wdown = w_mlp_up.astype(BF16), w_mlp_down.astype(BF16)
    nmix, nmlp = rows(norm_mix_w), rows(norm_mlp_w)
    cwa, cwb, cbb = conv_a_w.astype(F32), ssm_conv_w.astype(F32), rows(ssm_conv_b)
    dtb, alog = _pad_lanes(ssm_dt_bias, SMALL_DT_LANE), _pad_lanes(ssm_a_log, SMALL_DT_LANE)
    drow, snw = rows(jnp.repeat(ssm_d, SSM_HEADDIM, axis=-1)), rows(ssm_norm_w)
    wgk = jnp.pad(gla_w_gk2.astype(F32),
                  ((0, 0), (SMALL_GK_LANE, LANES - SMALL_GK_LANE - GLA_GATE_RANK), (0, 0)))
    bgk, gnw = rows(gla_b_gk), rows(gla_norm_w)
    nf = norm_f_w.reshape(1, -1).astype(F32)

    for i in range(depth):
        u, usmall = _proj(x2, nmix[i], wbig[i], wsmall[i], cwa[i], cwb[i], cbb[i], tm=tm_proj, seq=s)
        x2 = _mixer(x2, u, usmall, dtb[i], alog[i], drow[i], snw[i], wgk[i], bgk[i], gnw[i],
                    wa[i], wb[i], wc[i], wo[i], ts=ts_mix, batch=b, seq=s)
        x2 = _mlp(x2, nmlp[i], wup[i], wdown[i], nf, tm=tm_mlp, final_norm=(i == depth - 1))
    return x2.reshape(b, s, d)
```

```python
import functools

import jax
import jax.numpy as jnp
from jax import lax
from jax.experimental import pallas as pl
from jax.experimental.pallas import tpu as pltpu

F32 = jnp.float32
BF16 = jnp.bfloat16

EPS = 1e-6
D_MODEL = 1024
SC_WIDTH = 1024
SC_CONV_WIDTH = 3
SSM_D_INNER = 1024
SSM_HEADDIM = 64
SSM_HEADS = SSM_D_INNER // SSM_HEADDIM
SSM_GROUPS = 4
SSM_HPG = SSM_HEADS // SSM_GROUPS
SSM_STATE = 128
SSM_CONV_WIDTH = 4
SSM_CHUNK = 128
SSM_CONV_DIM = SSM_D_INNER + 2 * SSM_GROUPS * SSM_STATE
SSM_GROUP_WIDTH = SSM_D_INNER // SSM_GROUPS
GLA_HEADS = 4
GLA_KEY_DIM = D_MODEL // 2
GLA_VAL_DIM = D_MODEL
GLA_HEAD_K = GLA_KEY_DIM // GLA_HEADS
GLA_HEAD_V = GLA_VAL_DIM // GLA_HEADS
GLA_GATE_RANK = 16
GLA_GATE_NORMALIZER = 16.0
GLA_CHUNK = 64
N_BRANCHES = 3
D_FF = 4 * D_MODEL

LANES = 128
SUBLANES = 8
VMEM_LIMIT_BYTES = 56 * 1024 * 1024

PROJ_BLOCK = 1024
PB_AX, PB_AB, PB_AC, PB_Z, PB_XBC, PB_QK, PB_V, PB_G, PB_GATES = 0, 1, 2, 3, 4, 6, 7, 8, 9
N_PROJ_BLOCKS = 12
U_VA, U_Z, U_XBC, U_Q, U_K, U_V, U_G, U_GATES = 0, 1024, 2048, 4096, 4608, 5120, 6144, 7168
U_WIDTH = 10240
SMALL_DT_LANE = 0
SMALL_GK_LANE = GLA_GATE_RANK


def _split2(x):
    hi = x.astype(BF16)
    lo = (x - hi.astype(F32)).astype(BF16)
    return hi, lo


def _split3(x):
    hi = x.astype(BF16)
    r = x - hi.astype(F32)
    mid = r.astype(BF16)
    lo = (r - mid.astype(F32)).astype(BF16)
    return hi, mid, lo


def _dot(a, b):
    return jnp.dot(a, b, preferred_element_type=F32)


def _dot_nt(a, b):
    return lax.dot_general(a, b, (((1,), (1,)), ((), ())), preferred_element_type=F32)


def _dot_tn(a, b):
    return lax.dot_general(a, b, (((0,), (0,)), ((), ())), preferred_element_type=F32)


def _dot_exact_rhs(x, m):
    hi, mid, lo = _split3(x)
    return _dot(hi, m) + _dot(mid, m) + _dot(lo, m)


def _dot_exact_lhs(m, x):
    hi, mid, lo = _split3(x)
    return _dot(m, hi) + _dot(m, mid) + _dot(m, lo)


def _sigmoid(x):
    return 0.5 * jnp.tanh(0.5 * x) + 0.5


def _silu(x):
    h = 0.5 * x
    return h * jnp.tanh(h) + h


def _softplus(x):
    return jnp.maximum(x, 0.0) + jnp.log1p(jnp.exp(-jnp.abs(x)))


def _log_sigmoid(x):
    return jnp.minimum(x, 0.0) - jnp.log1p(jnp.exp(-jnp.abs(x)))


def _rms(x):
    return x * lax.rsqrt(jnp.mean(x * x, axis=-1, keepdims=True) + EPS)


def _proj_kernel(x_ref, nw_ref, w_ref, wsmall_ref, cwa_ref, cwb_ref, cbb_ref, u_ref, usmall_ref,
                 work_ref, h_ref, *, tiles_per_seq):
    tm = x_ref.shape[0]
    first = (pl.program_id(0) % tiles_per_seq) == 0
    n_conv, n_slabs = work_ref.shape[0], work_ref.shape[1]

    @pl.when(first)
    def _():
        work_ref[:, :, 0:SUBLANES, :] = jnp.zeros((n_conv, n_slabs, SUBLANES, LANES), F32)

    @pl.when(jnp.logical_not(first))
    def _():
        work_ref[:, :, 0:SUBLANES, :] = work_ref[:, :, tm:tm + SUBLANES, :]

    h_ref[...] = (_rms(x_ref[...]) * nw_ref[...]).astype(BF16)
    usmall_ref[...] = _dot(h_ref[...], wsmall_ref[...])

    def proj(blk):
        return _dot(h_ref[...], w_ref[:, blk * PROJ_BLOCK:(blk + 1) * PROJ_BLOCK])

    def causal_conv(slot, pre, cw, width):
        for sl in range(n_slabs):
            work_ref[slot, sl, SUBLANES:SUBLANES + tm, :] = pre[:, sl * LANES:(sl + 1) * LANES]
        y = None
        for k in range(width):
            start = SUBLANES - (width - 1) + k
            shifted = jnp.concatenate(
                [work_ref[slot, sl, pl.ds(start, tm, stride=1), :] for sl in range(n_slabs)], axis=1)
            term = shifted * cw[k:k + 1]
            y = term if y is None else y + term
        return y

    def put(col, val):
        u_ref[:, col:col + PROJ_BLOCK] = val.astype(BF16)

    ax = proj(PB_AX)
    ac = proj(PB_AC)
    conv_a = causal_conv(0, ac * ax, cwa_ref[...], SC_CONV_WIDTH)
    put(U_VA, proj(PB_AB) * conv_a)
    put(U_Z, _silu(proj(PB_Z)))
    cwb = cwb_ref[...]
    cbb = cbb_ref[...]
    for c in range(SSM_CONV_DIM // PROJ_BLOCK):
        cols = slice(c * PROJ_BLOCK, (c + 1) * PROJ_BLOCK)
        y = causal_conv(1 + c, proj(PB_XBC + c), cwb[:, cols], SSM_CONV_WIDTH)
        put(U_XBC + c * PROJ_BLOCK, _silu(y + cbb[:, cols]))
    put(U_Q, proj(PB_QK))
    put(U_V, proj(PB_V))
    put(U_G, _silu(proj(PB_G)))
    for c in range(N_BRANCHES):
        put(U_GATES + c * PROJ_BLOCK, _sigmoid(proj(PB_GATES + c)))


def _resident(shape):
    return pl.BlockSpec(shape, lambda *_: (0,) * len(shape), pipeline_mode=pl.Buffered(1))


def _proj(x2, norm_w, wbig, wsmall, conv_a_w, conv_b_w, conv_b_b, *, tm, seq):
    m = x2.shape[0]
    n_conv = 1 + SSM_CONV_DIM // PROJ_BLOCK
    return pl.pallas_call(
        functools.partial(_proj_kernel, tiles_per_seq=seq // tm),
        grid=(m // tm,),
        in_specs=[
            pl.BlockSpec((tm, D_MODEL), lambda i: (i, 0)),
            _resident((1, D_MODEL)),
            _resident((D_MODEL, N_PROJ_BLOCKS * PROJ_BLOCK)),
            _resident((D_MODEL, LANES)),
            _resident((SC_CONV_WIDTH, SC_WIDTH)),
            _resident((SSM_CONV_WIDTH, SSM_CONV_DIM)),
            _resident((1, SSM_CONV_DIM)),
        ],
        out_specs=[
            pl.BlockSpec((tm, U_WIDTH), lambda i: (i, 0)),
            pl.BlockSpec((tm, LANES), lambda i: (i, 0)),
        ],
        out_shape=[
            jax.ShapeDtypeStruct((m, U_WIDTH), BF16),
            jax.ShapeDtypeStruct((m, LANES), F32),
        ],
        scratch_shapes=[pltpu.VMEM((n_conv, PROJ_BLOCK // LANES, tm + SUBLANES, LANES), F32),
                        pltpu.VMEM((tm, D_MODEL), BF16)],
        compiler_params=pltpu.CompilerParams(
            dimension_semantics=("arbitrary",),
            vmem_limit_bytes=VMEM_LIMIT_BYTES),
        name="proj",
    )(x2, norm_w, wbig, wsmall, conv_a_w, conv_b_w, conv_b_b)


def _interleave(streams):
    streams = list(streams)
    while streams:
        for st in list(streams):
            try:
                next(st)
            except StopIteration:
                streams.remove(st)


def _ssd_consts(alog_ref):
    q = SSM_CHUNK
    row = lax.broadcasted_iota(jnp.int32, (q, q), 0)
    colm = lax.broadcasted_iota(jnp.int32, (q, q), 1)
    causal = row >= colm
    ltri = jnp.where(causal, 1.0, 0.0).astype(BF16)
    utri = jnp.where(row <= colm, 1.0, 0.0).astype(BF16)
    hrow = lax.broadcasted_iota(jnp.int32, (2 * LANES, SSM_D_INNER), 0) % LANES
    hcol = lax.broadcasted_iota(jnp.int32, (2 * LANES, SSM_D_INNER), 1)
    expand2 = jnp.where(hcol // SSM_HEADDIM == hrow, 1.0, 0.0).astype(BF16)
    lane = lax.broadcasted_iota(jnp.int32, (1, LANES), 1)
    a_full = jnp.where(lane < SSM_HEADS, -jnp.exp(alog_ref[...]), 0.0)
    glane = lax.broadcasted_iota(jnp.int32, (1, SSM_GROUP_WIDTH), 1)
    return causal, ltri, utri, expand2, a_full, glane


def _ssd_chunk(r0, consts, xbc_ref, zs_ref, small_ref, dtb, drow, nw, state_ref, y_ref):
    q = SSM_CHUNK
    gw = SSM_GROUP_WIDTH
    ns = SSM_STATE
    causal, ltri, utri, expand2, a_full, glane = consts

    dt = _softplus(small_ref[pl.ds(r0, q), :] + dtb)
    dta = dt * a_full
    acol = _dot_exact_lhs(ltri, dta)
    arow = _dot_exact_rhs(dta.T, utri)
    dt_row = dt.T
    yield
    alast = acol[q - 1:q, :]
    ea_cat = jnp.concatenate(_split2(jnp.exp(acol)), axis=1)
    w_cat = jnp.concatenate(_split2(dt * jnp.exp(alast - acol)), axis=1)
    yield

    def expand_group(g):
        gcols = slice(g * gw, (g + 1) * gw)
        return _dot(ea_cat, expand2[:, gcols]), _dot(w_cat, expand2[:, gcols])

    nxt = expand_group(0)
    for g in range(SSM_GROUPS):
        gcols = slice(g * gw, (g + 1) * gw)
        ea_g, w_g = nxt
        if g + 1 < SSM_GROUPS:
            nxt = expand_group(g + 1)
        xs_g = xbc_ref[pl.ds(r0, q), gcols]
        xs_f = xs_g.astype(F32)
        bm_g = xbc_ref[pl.ds(r0, q), SSM_D_INNER + g * ns:SSM_D_INNER + (g + 1) * ns]
        cm_g = xbc_ref[pl.ds(r0, q), SSM_D_INNER + (SSM_GROUPS + g) * ns:
                       SSM_D_INNER + (SSM_GROUPS + g + 1) * ns]
        cbm = _dot_nt(cm_g, bm_g)
        st = state_ref[g]
        y_off = _dot(cm_g, st.astype(BF16)) * ea_g
        state_ref[g] = st * ea_g[q - 1:q, :] + _dot_tn(bm_g, (xs_f * w_g).astype(BF16))
        yield
        lhs, rhs = [], []
        for r in range(SSM_HPG):
            h = g * SSM_HPG + r
            seg = acol[:, h:h + 1] - arow[h:h + 1, :]
            dec = jnp.where(causal, jnp.exp(seg), 0.0) * dt_row[h:h + 1, :]
            lhs.append((cbm * dec).astype(BF16))
            rhs.append(jnp.where(glane // SSM_HEADDIM == r, xs_g, jnp.zeros_like(xs_g)))
        y_diag = _dot(jnp.concatenate(lhs, axis=1), jnp.concatenate(rhs, axis=0))
        yield
        yg = y_diag + y_off + xs_f * drow[:, gcols]
        yg = _rms(yg * zs_ref[pl.ds(r0, q), gcols].astype(F32))
        y_ref[pl.ds(r0, q), gcols] = (yg * nw[:, gcols]).astype(BF16)


def _gla_chunk(r0, causal, ltri, q_ref, k_ref, v_ref, gs_ref, gk_ref, nw, state_ref, o_ref):
    t = GLA_CHUNK
    dk, dv = GLA_HEAD_K, GLA_HEAD_V
    scale = dk ** -0.5
    gcum = _dot_exact_lhs(ltri, gk_ref[pl.ds(r0, t), :])
    yield
    glast = gcum[t - 1:t, :]
    qf = q_ref[pl.ds(r0, t), :].astype(F32)
    kf = k_ref[pl.ds(r0, t), :].astype(F32)
    q_in = (qf * scale * jnp.exp(gcum)).astype(BF16)
    k_in = (kf * jnp.exp(-gcum)).astype(BF16)
    k_end = (kf * jnp.exp(glast - gcum)).astype(BF16)
    cdec = jnp.exp(glast)
    for h in range(GLA_HEADS):
        kcols = slice(h * dk, (h + 1) * dk)
        vcols = slice(h * dv, (h + 1) * dv)
        qh = q_in[:, kcols]
        vh = v_ref[pl.ds(r0, t), vcols]
        sc = _dot_nt(qh, k_in[:, kcols])
        st = state_ref[h]
        o_inter = _dot_nt(qh, st.astype(BF16))
        state_ref[h] = st * cdec[:, kcols] + _dot_tn(vh, k_end[:, kcols])
        yield
        o = _dot(jnp.where(causal, sc, 0.0).astype(BF16), vh) + o_inter
        yield
        o = _rms(o) * nw * gs_ref[pl.ds(r0, t), vcols].astype(F32)
        o_ref[pl.ds(r0, t), vcols] = o.astype(BF16)


def _mixer_kernel(x_ref, va_ref, zs_ref, xbc_ref, q_ref, k_ref, v_ref, gs_ref, ga_ref, gb_ref, gc_ref,
                  small_ref, dtb_ref, alog_ref, drow_ref, snw_ref, wgk_ref, bgk_ref, gnw_ref,
                  wa_ref, wb_ref, wc_ref, wo_ref, out_ref, sstate_ref, gstate_ref, y_ref, o_ref, gk_ref):
    ts = x_ref.shape[0]

    @pl.when(pl.program_id(1) == 0)
    def _():
        sstate_ref[...] = jnp.zeros_like(sstate_ref)
        gstate_ref[...] = jnp.zeros_like(gstate_ref)

    w_hi, w_lo = _split2(wgk_ref[...])
    s_hi, s_lo = _split2(small_ref[...])
    zg = _dot(s_hi, w_hi) + _dot(s_hi, w_lo) + _dot(s_lo, w_hi) + bgk_ref[...]
    gk_ref[...] = _log_sigmoid(zg) / GLA_GATE_NORMALIZER

    consts = _ssd_consts(alog_ref)
    t = GLA_CHUNK
    grow = lax.broadcasted_iota(jnp.int32, (t, t), 0)
    gcol = lax.broadcasted_iota(jnp.int32, (t, t), 1)
    gcausal = grow >= gcol
    gltri = jnp.where(gcausal, 1.0, 0.0).astype(BF16)
    dtb = dtb_ref[...]
    drow = drow_ref[...]
    snw = snw_ref[...]
    gnw = gnw_ref[...]
    gla_per_ssd = SSM_CHUNK // GLA_CHUNK

    def step(c, carry):
        r0 = pl.multiple_of(c * SSM_CHUNK, SSM_CHUNK)

        def gla_stream():
            for sub in range(gla_per_ssd):
                yield from _gla_chunk(pl.multiple_of(r0 + sub * t, t), gcausal, gltri, q_ref, k_ref, v_ref,
                                      gs_ref, gk_ref, gnw, gstate_ref, o_ref)

        _interleave([_ssd_chunk(r0, consts, xbc_ref, zs_ref, small_ref, dtb, drow, snw, sstate_ref, y_ref),
                     gla_stream()])
        return carry

    lax.fori_loop(0, ts // SSM_CHUNK, step, 0)

    merged = (ga_ref[...].astype(F32) * _dot(va_ref[...], wa_ref[...])
              + gb_ref[...].astype(F32) * _dot(y_ref[...], wb_ref[...])
              + gc_ref[...].astype(F32) * _dot(o_ref[...], wc_ref[...]))
    out_ref[...] = x_ref[...] + _dot(merged.astype(BF16), wo_ref[...])


def _mixer(x2, u, usmall, dt_bias, a_log, d_row, ssm_nw, wgk_full, b_gk, gla_nw, wa, wb, wc, wo,
           *, ts, batch, seq):
    m = x2.shape[0]
    tps = seq // ts

    def rows(w, c):
        return pl.BlockSpec((ts, w), lambda b, j: (b * tps + j, c // w))

    return pl.pallas_call(
        _mixer_kernel,
        grid=(batch, tps),
        in_specs=[
            rows(D_MODEL, 0),
            rows(SC_WIDTH, U_VA),
            rows(SSM_D_INNER, U_Z),
            rows(SSM_CONV_DIM, U_XBC),
            rows(GLA_KEY_DIM, U_Q),
            rows(GLA_KEY_DIM, U_K),
            rows(GLA_VAL_DIM, U_V),
            rows(GLA_VAL_DIM, U_G),
            rows(D_MODEL, U_GATES),
            rows(D_MODEL, U_GATES + D_MODEL),
            rows(D_MODEL, U_GATES + 2 * D_MODEL),
            rows(LANES, 0),
            _resident((1, LANES)),
            _resident((1, LANES)),
            _resident((1, SSM_D_INNER)),
            _resident((1, SSM_D_INNER)),
            _resident((LANES, GLA_KEY_DIM)),
            _resident((1, GLA_KEY_DIM)),
            _resident((1, GLA_HEAD_V)),
            _resident((SC_WIDTH, D_MODEL)),
            _resident((SSM_D_INNER, D_MODEL)),
            _resident((GLA_VAL_DIM, D_MODEL)),
            _resident((D_MODEL, D_MODEL)),
        ],
        out_specs=pl.BlockSpec((ts, D_MODEL), lambda b, j: (b * tps + j, 0)),
        out_shape=jax.ShapeDtypeStruct((m, D_MODEL), F32),
        scratch_shapes=[
            pltpu.VMEM((SSM_GROUPS, SSM_STATE, SSM_GROUP_WIDTH), F32),
            pltpu.VMEM((GLA_HEADS, GLA_HEAD_V, GLA_HEAD_K), F32),
            pltpu.VMEM((ts, SSM_D_INNER), BF16),
            pltpu.VMEM((ts, GLA_VAL_DIM), BF16),
            pltpu.VMEM((ts, GLA_KEY_DIM), F32),
        ],
        compiler_params=pltpu.CompilerParams(
            dimension_semantics=("arbitrary", "arbitrary"),
            vmem_limit_bytes=VMEM_LIMIT_BYTES),
        name="mixer",
    )(x2, u, u, u, u, u, u, u, u, u, u, usmall, dt_bias, a_log, d_row, ssm_nw, wgk_full, b_gk, gla_nw,
      wa, wb, wc, wo)


def _mlp_kernel(x_ref, nw_ref, wup_ref, wdown_ref, nf_ref, out_ref, *, ff_chunk, final_norm):
    x = x_ref[...]
    hb = (_rms(x) * nw_ref[...]).astype(BF16)
    acc = x
    for c in range(D_FF // ff_chunk):
        up = _dot(hb, wup_ref[:, c * ff_chunk:(c + 1) * ff_chunk])
        act = jnp.square(jnp.maximum(up, 0.0)).astype(BF16)
        acc = acc + _dot(act, wdown_ref[c * ff_chunk:(c + 1) * ff_chunk, :])
    if final_norm:
        acc = _rms(acc) * nf_ref[...]
    out_ref[...] = acc


def _mlp(x2, norm_w, wup, wdown, norm_f, *, tm, final_norm):
    m = x2.shape[0]
    return pl.pallas_call(
        functools.partial(_mlp_kernel, ff_chunk=1024, final_norm=final_norm),
        grid=(m // tm,),
        in_specs=[
            pl.BlockSpec((tm, D_MODEL), lambda i: (i, 0)),
            _resident((1, D_MODEL)),
            _resident((D_MODEL, D_FF)),
            _resident((D_FF, D_MODEL)),
            _resident((1, D_MODEL)),
        ],
        out_specs=pl.BlockSpec((tm, D_MODEL), lambda i: (i, 0)),
        out_shape=jax.ShapeDtypeStruct((m, D_MODEL), F32),
        compiler_params=pltpu.CompilerParams(
            dimension_semantics=("arbitrary",),
            vmem_limit_bytes=VMEM_LIMIT_BYTES),
        name="mlp",
    )(x2, norm_w, wup, wdown, norm_f)


def _pick(n, pref):
    t = min(pref, n)
    while n % t:
        t //= 2
    return t


def _reorder_w_in(w):
    o_dt = 3 * SC_WIDTH + SSM_D_INNER + SSM_CONV_DIM
    o_q = o_dt + SSM_HEADS
    o_gk = o_q + 2 * GLA_KEY_DIM + 2 * GLA_VAL_DIM
    o_gates = o_gk + GLA_GATE_RANK
    w = w.astype(BF16)
    big = jnp.concatenate([w[..., :o_dt], w[..., o_q:o_gk], w[..., o_gates:]], axis=-1)
    pad = jnp.zeros(w.shape[:-1] + (LANES - SSM_HEADS - GLA_GATE_RANK,), w.dtype)
    small = jnp.concatenate([w[..., o_dt:o_q], w[..., o_gk:o_gates], pad], axis=-1)
    return big, small


def _pad_lanes(v, offset):
    v = v.astype(F32)
    return jnp.pad(v, ((0, 0), (offset, LANES - offset - v.shape[-1])))[:, None, :]


def kernel(x, norm_mix_w, w_in, conv_a_w, w_out_a, ssm_conv_w, ssm_conv_b, ssm_dt_bias, ssm_a_log, ssm_d,
           ssm_norm_w, w_out_ssm, gla_w_gk2, gla_b_gk, gla_norm_w, w_out_gla, w_o, norm_mlp_w, w_mlp_up,
           w_mlp_down, norm_f_w):
    b, s, d = x.shape
    m = b * s
    depth = w_in.shape[0]
    x2 = x.reshape(m, d).astype(F32)
    tm_proj = _pick(s, 256)
    ts_mix = _pick(s, 512)
    tm_mlp = _pick(m, 1024)
    rows = lambda v: v.astype(F32)[:, None, :]

    wbig, wsmall = _reorder_w_in(w_in)
    wa, wb, wc, wo = (w.astype(BF16) for w in (w_out_a, w_out_ssm, w_out_gla, w_o))
    wup, wdown = w_mlp_up.astype(BF16), w_mlp_down.astype(BF16)
    nmix, nmlp = rows(norm_mix_w), rows(norm_mlp_w)
    cwa, cwb, cbb = conv_a_w.astype(F32), ssm_conv_w.astype(F32), rows(ssm_conv_b)
    dtb, alog = _pad_lanes(ssm_dt_bias, SMALL_DT_LANE), _pad_lanes(ssm_a_log, SMALL_DT_LANE)
    drow, snw = rows(jnp.repeat(ssm_d, SSM_HEADDIM, axis=-1)), rows(ssm_norm_w)
    wgk = jnp.pad(gla_w_gk2.astype(F32),
                  ((0, 0), (SMALL_GK_LANE, LANES - SMALL_GK_LANE - GLA_GATE_RANK), (0, 0)))
    bgk, gnw = rows(gla_b_gk), rows(gla_norm_w)
    nf = norm_f_w.reshape(1, -1).astype(F32)

    for i in range(depth):
        u, usmall = _proj(x2, nmix[i], wbig[i], wsmall[i], cwa[i], cwb[i], cbb[i], tm=tm_proj, seq=s)
        x2 = _mixer(x2, u, usmall, dtb[i], alog[i], drow[i], snw[i], wgk[i], bgk[i], gnw[i],
                    wa[i], wb[i], wc[i], wo[i], ts=ts_mix, batch=b, seq=s)
        x2 = _mlp(x2, nmlp[i], wup[i], wdown[i], nf, tm=tm_mlp, final_norm=(i == depth - 1))
    return x2.reshape(b, s, d)
```
